```python
import math
import jax, jax.numpy as jnp
from jax import lax
import numpy as np

D_MODEL = 1024
BATCH = 4
SEQ = 4096
DEPTH = 2

MIX_WIDTH = D_MODEL
RWKV_WIDTH = MIX_WIDTH // 2
HEAD_SIZE = 64
RWKV_HEADS = RWKV_WIDTH // HEAD_SIZE
DECAY_LORA = 64
AAA_LORA = 64
MV_LORA = 32
GATE_LORA = 128
RWKV_COLS = 3 * RWKV_WIDTH + DECAY_LORA + AAA_LORA + GATE_LORA
POOL_WIDTH = MIX_WIDTH - RWKV_WIDTH
POOL_WINDOWS = (2, 4, 8, 16)
POOL_GROUP = POOL_WIDTH // len(POOL_WINDOWS)
IN_COLS = RWKV_COLS + POOL_WIDTH
GN_EPS = 64e-5
NORM_EPS = 1e-6
N_KEYS = 128
N_EXPERTS = N_KEYS * N_KEYS
PEER_HEADS = 8
PEER_QDIM = 256
PEER_HALF = PEER_QDIM // 2
PEER_TOPK = 16
TOK_BLOCK = 128

kernel_name = "hybrid_rwkv7_pool_peer_adaln"


def rmsnorm(x, g):
    x32 = x.astype(jnp.float32)
    y = x32 * lax.rsqrt(jnp.mean(x32 * x32, axis=-1, keepdims=True) + NORM_EPS)
    return (y * g.astype(jnp.float32)).astype(x.dtype)


def modulate(h, shift, scale):
    return h * (1 + scale[:, None, :]) + shift[:, None, :]


def token_shift(z, mu):
    prev = jnp.pad(z, ((0, 0), (1, 0), (0, 0)))[:, :-1]
    return z + (prev - z) * mu


def rwkv7_step(state, inp):
    r_t, w_t, k_t, v_t, kk_t, a_t = inp
    sa = jnp.einsum('bhvk,bhk->bhv', state, -kk_t)
    state = (state * w_t[:, :, None, :]
             + sa[..., None] * (kk_t * a_t)[:, :, None, :]
             + v_t[..., None] * k_t[:, :, None, :])
    y = jnp.einsum('bhvk,bhk->bhv', state, r_t)
    return state, y


def rwkv7_mix(r, k, v, wd, ad, gd, w0, w_up, a0, a_up, g_up, k_k, k_a, r_k, lnx_g, lnx_b):
    B, S, _ = r.shape
    f32 = jnp.float32
    w = -jax.nn.softplus(-(w0 + jnp.tanh(wd) @ w_up)) - 0.5
    decay = jnp.exp(-jnp.exp(w.astype(f32)))
    a = jax.nn.sigmoid(a0 + ad @ a_up)
    g = jax.nn.sigmoid(gd) @ g_up

    def heads(t):
        return t.reshape(B, S, RWKV_HEADS, HEAD_SIZE).astype(f32)

    kk = heads(k * k_k)
    kk = kk / jnp.maximum(jnp.sqrt(jnp.sum(kk * kk, axis=-1, keepdims=True)), 1e-12)
    k = k * (1 + (a - 1) * k_a)
    rh, kh, vh, ah, wh = heads(r), heads(k), heads(v), heads(a), heads(decay)
    tm = lambda t: jnp.moveaxis(t, 1, 0)
    s0 = jnp.zeros((B, RWKV_HEADS, HEAD_SIZE, HEAD_SIZE), f32)
    _, ys = lax.scan(rwkv7_step, s0, (tm(rh), tm(wh), tm(kh), tm(vh), tm(kk), tm(ah)))
    y = jnp.moveaxis(ys, 0, 1)
    mean = jnp.mean(y, axis=-1, keepdims=True)
    var = jnp.mean(jnp.square(y - mean), axis=-1, keepdims=True)
    yn = ((y - mean) * lax.rsqrt(var + GN_EPS)).reshape(B, S, RWKV_WIDTH)
    yn = yn * lnx_g.astype(f32) + lnx_b.astype(f32)
    bonus = jnp.sum(rh * kh * r_k.astype(f32), axis=-1, keepdims=True) * vh
    out = (yn + bonus.reshape(B, S, RWKV_WIDTH)) * g.astype(f32)
    return out.astype(r.dtype)


def multiscale_pool(p, pool_w, pool_scale):
    B, S, _ = p.shape
    f32 = jnp.float32
    p32 = p.astype(f32)
    cs = jnp.cumsum(p32, axis=1)
    pos = jnp.arange(1, S + 1, dtype=f32)
    groups = []
    for j, win in enumerate(POOL_WINDOWS):
        sl = slice(j * POOL_GROUP, (j + 1) * POOL_GROUP)
        csj = cs[..., sl]
        lag = jnp.pad(csj, ((0, 0), (win, 0), (0, 0)))[:, :S]
        mean = (csj - lag) / jnp.minimum(pos, float(win))[None, :, None]
        groups.append(mean - p32[..., sl])
    d = jnp.stack(groups, axis=2)
    out = jnp.einsum('bsgc,gcd->bsgd', d, pool_w.astype(f32)).reshape(B, S, POOL_WIDTH)
    return (out * pool_scale.astype(f32)).astype(p.dtype)


def peer_ffn(h, peer_q, peer_keys, peer_u, peer_v):
    B, S, D = h.shape
    q = (h @ peer_q).reshape(B, S, PEER_HEADS, 2, PEER_HALF)
    scores = jnp.einsum('bshpd,hpkd->bshpk', q, peer_keys)
    s, i = lax.top_k(scores, PEER_TOPK)
    cand_s = (s[..., 0, :, None] + s[..., 1, None, :]).reshape(B, S, PEER_HEADS, PEER_TOPK * PEER_TOPK)
    cand_i = (i[..., 0, :, None] * N_KEYS + i[..., 1, None, :]).reshape(B, S, PEER_HEADS, PEER_TOPK * PEER_TOPK)
    top_s, sel = lax.top_k(cand_s, PEER_TOPK)
    idx = jnp.take_along_axis(cand_i, sel, axis=-1)
    gate = jax.nn.softmax(top_s.astype(jnp.float32), axis=-1)
    T = B * S
    E = PEER_HEADS * PEER_TOPK
    nb = T // TOK_BLOCK
    hb = h.reshape(nb, TOK_BLOCK, D)
    ib = idx.reshape(nb, TOK_BLOCK, E)
    gb = gate.astype(h.dtype).reshape(nb, TOK_BLOCK, E)

    def block(args):
        hx, ix, gx = args
        u_sel = jnp.take(peer_u, ix, axis=0)
        v_sel = jnp.take(peer_v, ix, axis=0)
        act = jax.nn.gelu(jnp.einsum('td,ted->te', hx, u_sel), approximate=False)
        return jnp.einsum('te,ted->td', gx * act, v_sel)

    out = lax.map(block, (hb, ib, gb))
    return out.reshape(B, S, D)


def setup_inputs(seed: int = 0) -> dict:
    key = jax.random.key(seed)
    ks = jax.random.split(key, 32)
    nrm = lambda k, shape, s: jax.random.normal(k, shape, jnp.float32) * s
    L, Lr = DEPTH, DEPTH - 1
    return {
        "x": nrm(ks[0], (BATCH, SEQ, D_MODEL), 1.0),
        "c": nrm(ks[1], (BATCH, D_MODEL), 1.0),
        "ada_w": nrm(ks[2], (L, D_MODEL, 6 * D_MODEL), 0.5 * D_MODEL ** -0.5),
        "ada_b": nrm(ks[3], (L, 6 * D_MODEL), 0.01),
        "ln1_g": 1.0 + nrm(ks[4], (L, D_MODEL), 0.01),
        "w_in": nrm(ks[5], (L, D_MODEL, IN_COLS), D_MODEL ** -0.5),
        "mu_shift": jax.random.uniform(ks[6], (L, RWKV_COLS), jnp.float32),
        "w0": nrm(ks[7], (L, RWKV_WIDTH), 0.5),
        "w_up": nrm(ks[8], (L, DECAY_LORA, RWKV_WIDTH), 0.1 * DECAY_LORA ** -0.5),
        "a0": nrm(ks[9], (L, RWKV_WIDTH), 0.1),
        "a_up": nrm(ks[10], (L, AAA_LORA, RWKV_WIDTH), 0.1 * AAA_LORA ** -0.5),
        "g_up": nrm(ks[11], (L, GATE_LORA, RWKV_WIDTH), GATE_LORA ** -0.5),
        "vres_down": nrm(ks[12], (Lr, D_MODEL, MV_LORA), D_MODEL ** -0.5),
        "vres_mu": jax.random.uniform(ks[13], (Lr, MV_LORA), jnp.float32),
        "vres_v0": 1.0 + nrm(ks[14], (Lr, RWKV_WIDTH), 0.1),
        "vres_up": nrm(ks[15], (Lr, MV_LORA, RWKV_WIDTH), 0.1 * MV_LORA ** -0.5),
        "k_k": 0.85 + nrm(ks[16], (L, RWKV_WIDTH), 0.02),
        "k_a": 1.0 + nrm(ks[17], (L, RWKV_WIDTH), 0.02),
        "r_k": -0.04 + nrm(ks[18], (L, RWKV_HEADS, HEAD_SIZE), 0.02),
        "lnx_g": 1.0 + nrm(ks[19], (L, RWKV_WIDTH), 0.01),
        "lnx_b": nrm(ks[20], (L, RWKV_WIDTH), 0.01),
        "pool_w": nrm(ks[21], (L, len(POOL_WINDOWS), POOL_GROUP, POOL_GROUP), POOL_GROUP ** -0.5),
        "pool_scale": 1.0 + nrm(ks[22], (L, POOL_WIDTH), 0.01),
        "w_out": nrm(ks[23], (L, MIX_WIDTH, D_MODEL), MIX_WIDTH ** -0.5),
        "ln2_g": 1.0 + nrm(ks[24], (L, D_MODEL), 0.01),
        "peer_q": nrm(ks[25], (L, D_MODEL, PEER_HEADS * PEER_QDIM), D_MODEL ** -0.5),
        "peer_keys": nrm(ks[26], (L, PEER_HEADS, 2, N_KEYS, PEER_HALF), PEER_HALF ** -0.5),
        "peer_u": nrm(ks[27], (L, N_EXPERTS, D_MODEL), D_MODEL ** -0.5),
        "peer_v": nrm(ks[28], (L, N_EXPERTS, D_MODEL), PEER_HEADS ** -0.5),
        "lnf_g": 1.0 + nrm(ks[29], (D_MODEL,), 0.01),
    }


def reference(x, c, ada_w, ada_b, ln1_g, w_in, mu_shift, w0, w_up, a0, a_up, g_up,
              vres_down, vres_mu, vres_v0, vres_up, k_k, k_a, r_k, lnx_g, lnx_b,
              pool_w, pool_scale, w_out, ln2_g, peer_q, peer_keys, peer_u, peer_v, lnf_g):
    split_pts = [RWKV_WIDTH, 2 * RWKV_WIDTH, 3 * RWKV_WIDTH,
                 3 * RWKV_WIDTH + DECAY_LORA, 3 * RWKV_WIDTH + DECAY_LORA + AAA_LORA]
    c_act = jax.nn.silu(c)
    v_first = None
    for l in range(DEPTH):
        mod = c_act @ ada_w[l] + ada_b[l]
        sh1, sc1, g1, sh2, sc2, g2 = jnp.split(mod, 6, axis=-1)

        h = modulate(rmsnorm(x, ln1_g[l]), sh1, sc1)
        w_full = w_in[l] if l == 0 else jnp.concatenate([w_in[l], vres_down[l - 1]], axis=1)
        proj = h @ w_full
        rw = token_shift(proj[..., :RWKV_COLS], mu_shift[l])
        r, k, v, wd, ad, gd = jnp.split(rw, split_pts, axis=-1)
        pool_in = proj[..., RWKV_COLS:RWKV_COLS + POOL_WIDTH]
        if l == 0:
            v_first = v
        else:
            vd = token_shift(proj[..., RWKV_COLS + POOL_WIDTH:], vres_mu[l - 1])
            v = v + (v_first - v) * jax.nn.sigmoid(vres_v0[l - 1] + vd @ vres_up[l - 1])
        y_rwkv = rwkv7_mix(r, k, v, wd, ad, gd, w0[l], w_up[l], a0[l], a_up[l], g_up[l],
                           k_k[l], k_a[l], r_k[l], lnx_g[l], lnx_b[l])
        y_pool = multiscale_pool(pool_in, pool_w[l], pool_scale[l])
        mix = jnp.concatenate([y_rwkv, y_pool], axis=-1) @ w_out[l]
        x = x + g1[:, None, :] * mix

        h2 = modulate(rmsnorm(x, ln2_g[l]), sh2, sc2)
        x = x + g2[:, None, :] * peer_ffn(h2, peer_q[l], peer_keys[l], peer_u[l], peer_v[l])
    return rmsnorm(x, lnf_g)
```

```python
import functools

import jax
import jax.numpy as jnp
from jax import lax
from jax.experimental import pallas as pl
from jax.experimental.pallas import tpu as pltpu

F32 = jnp.float32
BF16 = jnp.bfloat16
HIGHEST = lax.Precision.HIGHEST

HEAD_SIZE = 64
RWKV_WIDTH = 512
POOL_WIDTH = 512
POOL_GROUP = 128
POOL_WINDOWS = (2, 4, 8, 16)
MAX_WINDOW = 16
RWKV_COLS = 1792
LORA_COLS = 128
GATE_LORA = 128
VRES_PAD = 128
GN_EPS = 64e-5
NORM_EPS = 1e-6
N_KEYS = 128
PEER_HEADS = 8
PEER_TOPK = 16
CHUNK = 64
PAIR = 128

TM = 256
TS = 512
TQ = 512
TT = 512
EB = 512

VMEM_LIMIT = 56 * 1024 * 1024

_CAND_ROWS = [(a, PEER_TOPK // (a + 1)) for a in range(PEER_TOPK)]
_CAND_OFFS = []
_off = 0
for _a, _nb in _CAND_ROWS:
    _CAND_OFFS.append(_off)
    _off += _nb
N_CAND = _off
N_CAND_PAD = -(-N_CAND // 8) * 8


def _dot(a, b, dims=(((1,), (0,)), ((), ())), precision=None):
    return lax.dot_general(a, b, dims, precision=precision, preferred_element_type=F32)


_NT = (((1,), (1,)), ((), ()))
_TN = (((0,), (0,)), ((), ()))


def _group_ones(n, group):
    r = lax.broadcasted_iota(jnp.int32, (n, n), 0) // group
    c = lax.broadcasted_iota(jnp.int32, (n, n), 1) // group
    return (r == c).astype(F32)


def _rms_mod(x, g, shift, scale):
    y = x * lax.rsqrt(jnp.mean(x * x, axis=-1, keepdims=True) + NORM_EPS)
    return (y * g) * (1.0 + scale) + shift


def _ada_kernel(c_ref, w_ref, b_ref, o_ref):
    c = c_ref[...]
    ca = c * jax.nn.sigmoid(c)
    o_ref[0] = _dot(ca, w_ref[0], precision=HIGHEST) + b_ref[0]


def _ada_call(c_pad, ada_w, ada_b):
    L, D, D6 = ada_w.shape
    nb = D6 // D
    return pl.pallas_call(
        _ada_kernel,
        grid=(L, nb),
        in_specs=[
            pl.BlockSpec((8, D), lambda l, j: (0, 0)),
            pl.BlockSpec((1, D, D), lambda l, j: (l, 0, j)),
            pl.BlockSpec((1, 1, D), lambda l, j: (l, 0, j)),
        ],
        out_specs=pl.BlockSpec((1, 8, D), lambda l, j: (l, 0, j)),
        out_shape=jax.ShapeDtypeStruct((L, 8, D6), F32),
        name="adaln",
    )(c_pad, ada_w, ada_b.reshape(L, 1, D6))


def _premix_kernel(*refs, tiles_per_batch, has_vres):
    if has_vres:
        (x_ref, mod_ref, ln_ref, w_ref, mu_ref, w0_ref, a0_ref, lora_ref, gup_ref, kk_ref, ka_ref,
         vmu_ref, v0_ref, vup_ref, vfirst_ref,
         r_out, k_out, v_out, lw_out, kkn_out, a_out, g_out, p_out, carry_ref) = refs
    else:
        (x_ref, mod_ref, ln_ref, w_ref, mu_ref, w0_ref, a0_ref, lora_ref, gup_ref, kk_ref, ka_ref,
         r_out, k_out, v_out, lw_out, kkn_out, a_out, g_out, p_out, carry_ref) = refs
    i = pl.program_id(0)

    @pl.when(i % tiles_per_batch == 0)
    def _():
        carry_ref[...] = jnp.zeros_like(carry_ref)

    m = mod_ref[0]
    h = _rms_mod(x_ref[...], ln_ref[...], m[0:1], m[1:2])
    proj = _dot(h.astype(BF16), w_ref[...])
    tm = proj.shape[0]
    row0 = lax.broadcasted_iota(jnp.int32, (tm, 1), 0) == 0
    carry = carry_ref[...]
    carry_ref[...] = proj[tm - 1:tm, :]

    def shifted(lo, hi, mu):
        z = proj[:, lo:hi]
        prev = jnp.where(row0, carry[:, lo:hi], pltpu.roll(z, 1, 0))
        return z + (prev - z) * mu

    rw = shifted(0, RWKV_COLS, mu_ref[...])
    r = rw[:, 0:RWKV_WIDTH]
    k = rw[:, RWKV_WIDTH:2 * RWKV_WIDTH]
    v = rw[:, 2 * RWKV_WIDTH:3 * RWKV_WIDTH]
    lo = 3 * RWKV_WIDTH
    wa = rw[:, lo:lo + LORA_COLS]
    gd = rw[:, lo + LORA_COLS:lo + LORA_COLS + GATE_LORA]
    lane = lax.broadcasted_iota(jnp.int32, wa.shape, 1)
    wa = jnp.where(lane < LORA_COLS // 2, jnp.tanh(wa), wa)
    lora = _dot(wa, lora_ref[...], precision=HIGHEST)
    zw = -(w0_ref[...] + lora[:, 0:RWKV_WIDTH])
    softplus = jnp.maximum(zw, 0.0) + jnp.log1p(jnp.exp(-jnp.abs(zw)))
    w = -softplus - 0.5
    a = jax.nn.sigmoid(a0_ref[...] + lora[:, RWKV_WIDTH:2 * RWKV_WIDTH])
    g = _dot(jax.nn.sigmoid(gd), gup_ref[...], precision=HIGHEST)
    kk = k * kk_ref[...]
    ss = _dot(kk * kk, _group_ones(RWKV_WIDTH, HEAD_SIZE), precision=HIGHEST)
    kk = kk / jnp.maximum(jnp.sqrt(ss), 1e-12)
    k2 = k * (1.0 + (a - 1.0) * ka_ref[...])
    if has_vres:
        base = RWKV_COLS + POOL_WIDTH
        vd = shifted(base, base + VRES_PAD, vmu_ref[...])
        mixv = jax.nn.sigmoid(v0_ref[...] + _dot(vd, vup_ref[...], precision=HIGHEST))
        v = v + (vfirst_ref[...] - v) * mixv
    r_out[...] = r
    k_out[...] = k2
    v_out[...] = v
    lw_out[...] = -jnp.exp(w)
    kkn_out[...] = kk
    a_out[...] = a
    g_out[...] = g
    p_out[...] = proj[:, RWKV_COLS:RWKV_COLS + POOL_WIDTH]


def _premix_call(x2, modl, ln_g, w_full, mu, w0, a0, lora_w, g_up, k_k, k_a, vres, tiles_per_batch):
    T, D = x2.shape
    NC = w_full.shape[1]
    has_vres = vres is not None
    row = lambda n: pl.BlockSpec((1, n), lambda i: (0, 0))
    full = lambda a: pl.BlockSpec(a.shape, lambda i: (0,) * a.ndim)
    tok = lambda n: pl.BlockSpec((TM, n), lambda i: (i, 0))
    in_specs = [tok(D), pl.BlockSpec((1, 6, D), lambda i: (i // tiles_per_batch, 0, 0)), row(D),
                full(w_full), row(RWKV_COLS), row(RWKV_WIDTH), row(RWKV_WIDTH), full(lora_w), full(g_up),
                row(RWKV_WIDTH), row(RWKV_WIDTH)]
    args = [x2, modl, ln_g, w_full, mu, w0, a0, lora_w, g_up, k_k, k_a]
    if has_vres:
        vmu, v0, vup, vfirst = vres
        in_specs += [row(VRES_PAD), row(RWKV_WIDTH), full(vup), tok(RWKV_WIDTH)]
        args += [vmu, v0, vup, vfirst]
    out = jax.ShapeDtypeStruct((T, RWKV_WIDTH), F32)
    return pl.pallas_call(
        functools.partial(_premix_kernel, tiles_per_batch=tiles_per_batch, has_vres=has_vres),
        grid=(T // TM,),
        in_specs=in_specs,
        out_specs=[tok(RWKV_WIDTH)] * 8,
        out_shape=[out] * 8,
        scratch_shapes=[pltpu.VMEM((1, NC), F32)],
        compiler_params=pltpu.CompilerParams(dimension_semantics=("arbitrary",),
                                             vmem_limit_bytes=VMEM_LIMIT),
        name="premix",
    )(*args)


def _rwkv_kernel(r_ref, k_ref, v_ref, lw_ref, kk_ref, a_ref, g_ref, rk_ref, lng_ref, lnb_ref,
                 o_ref, state_ref, y_ref):
    @pl.when(pl.program_id(1) == 0)
    def _():
        state_ref[...] = jnp.zeros_like(state_ref)

    ts = r_ref.shape[0]
    n_pairs = r_ref.shape[1] // PAIR
    ti = lax.broadcasted_iota(jnp.int32, (CHUNK, CHUNK), 0)
    si = lax.broadcasted_iota(jnp.int32, (CHUNK, CHUNK), 1)
    tril = (ti >= si).astype(F32)
    ri = lax.broadcasted_iota(jnp.int32, (PAIR, PAIR), 0)
    ci = lax.broadcasted_iota(jnp.int32, (PAIR, PAIR), 1)
    strict = (ri % CHUNK) > (ci % CHUNK)
    incl = (ri % CHUNK) >= (ci % CHUNK)
    eye = (ri == ci).astype(F32)
    lane = lax.broadcasted_iota(jnp.int32, (CHUNK, PAIR), 1)
    head0 = lane < HEAD_SIZE

    def stack(t):
        return jnp.concatenate([jnp.where(head0, t, 0.0), jnp.where(head0, 0.0, t)], axis=0)

    def chunk_body(c, carry):
        rows = pl.ds(pl.multiple_of(c * CHUNK, CHUNK), CHUNK)
        lw = lw_ref[rows, :]
        cum = _dot(tril, lw, precision=HIGHEST)
        g_in = jnp.exp(cum)
        g_ex = jnp.exp(cum - lw)
        g_inv = jnp.exp(-cum)
        kk = kk_ref[rows, :]
        rt = r_ref[rows, :] * g_in
        kt = k_ref[rows, :] * g_inv
        bt = (kk * a_ref[rows, :]) * g_inv
        at = -kk * g_ex
        vv = v_ref[rows, :]
        for p in range(n_pairs):
            ls = slice(p * PAIR, (p + 1) * PAIR)
            xa, xb, xk, xr, xv = (stack(t[:, ls]).astype(BF16) for t in (at, bt, kt, rt, vv))
            lab = jnp.where(strict, _dot(xa, xb, _NT), 0.0)
            lak = jnp.where(strict, _dot(xa, xk, _NT), 0.0)
            lrb = jnp.where(incl, _dot(xr, xb, _NT), 0.0)
            lrk = jnp.where(incl, _dot(xr, xk, _NT), 0.0)
            tinv = eye + lab
            lp = lab
            for _ in range(5):
                lpb = lp.astype(BF16)
                lp = _dot(lpb, lpb)
                tinv = tinv + _dot(lp.astype(BF16), tinv.astype(BF16))
            tb = tinv.astype(BF16)
            lakv = _dot(lak.astype(BF16), xv)
            a2 = _dot(tb, xa)
            uv = _dot(tb, lakv.astype(BF16))
            lrbb = lrb.astype(BF16)
            r2 = xr.astype(F32) + _dot(lrbb, a2.astype(BF16))
            yv = _dot(lrbb, uv.astype(BF16)) + _dot(lrk.astype(BF16), xv)
            kv = _dot(xv, xk, _TN)
            pt = state_ref[p]
            ptb = pt.astype(BF16)
            y_st = _dot(r2.astype(BF16), ptb, _NT) + yv
            u_st = _dot(a2.astype(BF16), ptb, _NT) + uv
            st = _dot(u_st.astype(BF16), xb, _TN) + kv
            state_ref[p] = (pt + st) * g_in[CHUNK - 1:CHUNK, ls]
            y_ref[rows, ls] = y_st[0:CHUNK] + y_st[CHUNK:2 * CHUNK]
        return carry

    lax.fori_loop(0, ts // CHUNK, chunk_body, 0)

    ones = _group_ones(RWKV_WIDTH, HEAD_SIZE)
    y = y_ref[...]
    mean = _dot(y, ones, precision=HIGHEST) * (1.0 / HEAD_SIZE)
    d = y - mean
    var = _dot(d * d, ones, precision=HIGHEST) * (1.0 / HEAD_SIZE)
    yn = d * lax.rsqrt(var + GN_EPS) * lng_ref[...] + lnb_ref[...]
    rr = r_ref[...]
    bonus = _dot(rr * k_ref[...] * rk_ref[...], ones, precision=HIGHEST) * v_ref[...]
    o_ref[...] = (yn + bonus) * g_ref[...]


def _rwkv_call(r, k, v, lw, kk, a, g, r_k, lnx_g, lnx_b, batch):
    T, W = r.shape
    steps = T // batch // TS
    tok = pl.BlockSpec((TS, W), lambda b, j: (b * steps + j, 0))
    row = pl.BlockSpec((1, W), lambda b, j: (0, 0))
    return pl.pallas_call(
        _rwkv_kernel,
        grid=(batch, steps),
        in_specs=[tok] * 7 + [row] * 3,
        out_specs=tok,
        out_shape=jax.ShapeDtypeStruct((T, W), F32),
        scratch_shapes=[pltpu.VMEM((W // PAIR, PAIR, PAIR), F32), pltpu.VMEM((TS, W), F32)],
        compiler_params=pltpu.CompilerParams(dimension_semantics=("arbitrary", "arbitrary"),
                                             vmem_limit_bytes=VMEM_LIMIT),
        name="rwkv",
    )(r, k, v, lw, kk, a, g, r_k, lnx_g, lnx_b)


def _postmix_kernel(x_ref, y_ref, p_ref, mod_ref, pw_ref, ps_ref, wo_ref, ln_ref,
                    x_out, ht_out, carry_ref, *, tiles_per_batch):
    i = pl.program_id(0)
    j = i % tiles_per_batch

    @pl.when(j == 0)
    def _():
        carry_ref[...] = jnp.zeros_like(carry_ref)

    m = mod_ref[0]
    p = p_ref[...]
    tm = p.shape[0]
    ext = jnp.concatenate([carry_ref[...], p], axis=0)
    carry_ref[...] = p[tm - MAX_WINDOW:tm, :]
    pos = (j * tm + 1 + lax.broadcasted_iota(jnp.int32, (tm, 1), 0)).astype(F32)
    pooled = []
    for gi, win in enumerate(POOL_WINDOWS):
        ls = slice(gi * POOL_GROUP, (gi + 1) * POOL_GROUP)
        wsum = ext[:, ls]
        span = 1
        while span < win:
            wsum = wsum + pltpu.roll(wsum, span, 0)
            span *= 2
        mean = wsum[MAX_WINDOW:, :] / jnp.minimum(pos, float(win))
        d = mean - p[:, ls]
        pooled.append(_dot(d, pw_ref[gi], precision=HIGHEST))
    y_pool = jnp.concatenate(pooled, axis=1) * ps_ref[...]
    mix = (_dot(y_ref[...].astype(BF16), wo_ref[0:RWKV_WIDTH, :])
           + _dot(y_pool.astype(BF16), wo_ref[RWKV_WIDTH:RWKV_WIDTH + POOL_WIDTH, :]))
    x = x_ref[...] + m[2:3] * mix
    x_out[...] = x
    h2 = _rms_mod(x, ln_ref[...], m[3:4], m[4:5])
    ht_out[...] = h2.T.astype(BF16)


def _postmix_call(x2, y_rwkv, pool_in, modl, pool_w, pool_scale, w_out, ln_g, tiles_per_batch):
    T, D = x2.shape
    row = lambda n: pl.BlockSpec((1, n), lambda i: (0, 0))
    full = lambda a: pl.BlockSpec(a.shape, lambda i: (0,) * a.ndim)
    tok = lambda n: pl.BlockSpec((TM, n), lambda i: (i, 0))
    return pl.pallas_call(
        functools.partial(_postmix_kernel, tiles_per_batch=tiles_per_batch),
        grid=(T // TM,),
        in_specs=[tok(D), tok(RWKV_WIDTH), tok(POOL_WIDTH),
                  pl.BlockSpec((1, 6, D), lambda i: (i // tiles_per_batch, 0, 0)),
                  full(pool_w), row(POOL_WIDTH), full(w_out), row(D)],
        out_specs=[tok(D), pl.BlockSpec((D, TM), lambda i: (0, i))],
        out_shape=[jax.ShapeDtypeStruct((T, D), F32), jax.ShapeDtypeStruct((D, T), BF16)],
        scratch_shapes=[pltpu.VMEM((MAX_WINDOW, POOL_WIDTH), F32)],
        compiler_params=pltpu.CompilerParams(dimension_semantics=("arbitrary",),
                                             vmem_limit_bytes=VMEM_LIMIT),
        name="postmix",
    )(x2, y_rwkv, pool_in, modl, pool_w, pool_scale, w_out, ln_g)


def _top16(s, vals_ref):
    n, tq = s.shape
    iota = lax.broadcasted_iota(jnp.int32, (n, tq), 0).astype(F32)
    rank = jnp.full((n, tq), 99.0, F32)
    cur = s
    for j in range(PEER_TOPK):
        mx = jnp.max(cur, axis=0, keepdims=True)
        first = jnp.min(jnp.where(cur == mx, iota, float(n)), axis=0, keepdims=True)
        hit = iota == first
        rank = jnp.where(hit, float(j + 1), rank)
        cur = jnp.where(hit, -jnp.inf, cur)
        vals_ref[j:j + 1, :] = mx
    return rank


def _select_kernel(ht_ref, qt_ref, keys_ref, rank2_out, e2_out, b_out, f_out,
                   v1_ref, v2_ref, cand_ref, bm_ref):
    qt = _dot(qt_ref[...], ht_ref[...])
    s1 = _dot(keys_ref[0, 0], qt[0:N_KEYS], precision=HIGHEST)
    s2 = _dot(keys_ref[0, 1], qt[N_KEYS:2 * N_KEYS], precision=HIGHEST)
    rank1 = _top16(s1, v1_ref)
    rank2 = _top16(s2, v2_ref)
    tq = s1.shape[1]
    for (a, nb), off in zip(_CAND_ROWS, _CAND_OFFS):
        cand_ref[off:off + nb, :] = v1_ref[a:a + 1, :] + v2_ref[0:nb, :]
    cand_ref[N_CAND:N_CAND_PAD, :] = jnp.full((N_CAND_PAD - N_CAND, tq), -jnp.inf, F32)
    cand = cand_ref[...]
    iota = lax.broadcasted_iota(jnp.int32, cand.shape, 0).astype(F32)
    chosen = jnp.zeros(cand.shape, F32)
    cur = cand
    for _ in range(PEER_TOPK):
        mx = jnp.max(cur, axis=0, keepdims=True)
        first = jnp.min(jnp.where(cur == mx, iota, float(N_CAND_PAD)), axis=0, keepdims=True)
        hit = iota == first
        chosen = jnp.where(hit, 1.0, chosen)
        cur = jnp.where(hit, -jnp.inf, cur)
    z = jnp.sum(chosen * jnp.exp(cand - cand[0:1, :]), axis=0, keepdims=True)
    cand_ref[...] = chosen
    for (a, nb), off in zip(_CAND_ROWS, _CAND_OFFS):
        bm_ref[a:a + 1, :] = jnp.sum(cand_ref[off:off + nb, :], axis=0, keepdims=True)
    bsel = jnp.zeros(s1.shape, F32)
    for a in range(PEER_TOPK):
        bsel = jnp.where(rank1 == float(a + 1), bm_ref[a:a + 1, :], bsel)
    rank2_out[0] = rank2
    e2_out[0] = jnp.exp(s2 - v2_ref[0:1, :])
    b_out[0] = bsel
    f_out[0] = jnp.exp(s1 - v1_ref[0:1, :]) / z


def _select_call(h2t, qt_w, keys):
    D, T = h2t.shape
    out = jax.ShapeDtypeStruct((PEER_HEADS, N_KEYS, T), F32)
    ospec = pl.BlockSpec((1, N_KEYS, TQ), lambda i, h: (h, 0, i))
    return pl.pallas_call(
        _select_kernel,
        grid=(T // TQ, PEER_HEADS),
        in_specs=[pl.BlockSpec((D, TQ), lambda i, h: (0, i)),
                  pl.BlockSpec((2 * N_KEYS, D), lambda i, h: (h, 0)),
                  pl.BlockSpec((1, 2, N_KEYS, N_KEYS), lambda i, h: (h, 0, 0, 0))],
        out_specs=[ospec] * 4,
        out_shape=[out] * 4,
        scratch_shapes=[pltpu.VMEM((PEER_TOPK, TQ), F32), pltpu.VMEM((PEER_TOPK, TQ), F32),
                        pltpu.VMEM((N_CAND_PAD, TQ), F32), pltpu.VMEM((PEER_TOPK, TQ), F32)],
        compiler_params=pltpu.CompilerParams(dimension_semantics=("arbitrary", "arbitrary"),
                                             vmem_limit_bytes=VMEM_LIMIT),
        name="select",
    )(h2t, qt_w, keys)


def _experts_kernel(ht_ref, u_ref, vt_ref, rank2_ref, e2_ref, b_ref, f_ref, x_ref, mod_ref, lnf_ref,
                    x_out, acc_ref, *, tiles_per_batch, final):
    e = pl.program_id(1)

    @pl.when(e == 0)
    def _():
        acc_ref[...] = jnp.zeros_like(acc_ref)

    act = _dot(u_ref[...], ht_ref[...])
    gates = []
    for ii in range(EB // N_KEYS):
        i1 = e * (EB // N_KEYS) + ii
        gate = None
        for h in range(PEER_HEADS):
            brow = b_ref[h, pl.ds(i1, 1), :]
            frow = f_ref[h, pl.ds(i1, 1), :]
            term = jnp.where(rank2_ref[h] <= brow, e2_ref[h] * frow, 0.0)
            gate = term if gate is None else gate + term
        gates.append(gate)
    gate = jnp.concatenate(gates, axis=0)
    gelu = 0.5 * act * (1.0 + lax.erf(act * (2.0 ** -0.5)))
    z = (gate * gelu).astype(BF16)
    acc_ref[...] += _dot(vt_ref[...], z)

    @pl.when(e == pl.num_programs(1) - 1)
    def _():
        m = mod_ref[0]
        x = x_ref[...] + m[5:6] * acc_ref[...].T
        if final:
            x = x * lax.rsqrt(jnp.mean(x * x, axis=-1, keepdims=True) + NORM_EPS) * lnf_ref[...]
        x_out[...] = x


def _experts_call(h2t, u_bf, vt_bf, rank2, e2, bsel, fsel, x2, modl, lnf_g, tiles_per_batch, final):
    D, T = h2t.shape
    NE = u_bf.shape[0]
    sel = pl.BlockSpec((PEER_HEADS, N_KEYS, TT), lambda i, e: (0, 0, i))
    tok = pl.BlockSpec((TT, D), lambda i, e: (i, 0))
    return pl.pallas_call(
        functools.partial(_experts_kernel, tiles_per_batch=tiles_per_batch, final=final),
        grid=(T // TT, NE // EB),
        in_specs=[pl.BlockSpec((D, TT), lambda i, e: (0, i)),
                  pl.BlockSpec((EB, D), lambda i, e: (e, 0)),
                  pl.BlockSpec((D, EB), lambda i, e: (0, e)),
                  sel, sel, sel, sel, tok,
                  pl.BlockSpec((1, 6, D), lambda i, e: (i // tiles_per_batch, 0, 0)),
                  pl.BlockSpec((1, D), lambda i, e: (0, 0))],
        out_specs=tok,
        out_shape=jax.ShapeDtypeStruct((T, D), F32),
        scratch_shapes=[pltpu.VMEM((D, TT), F32)],
        compiler_params=pltpu.CompilerParams(dimension_semantics=("arbitrary", "arbitrary"),
                                             vmem_limit_bytes=VMEM_LIMIT),
        name="experts",
    )(h2t, u_bf, vt_bf, rank2, e2, bsel, fsel, x2, modl, lnf_g)


def kernel(x, c, ada_w, ada_b, ln1_g, w_in, mu_shift, w0, w_up, a0, a_up, g_up, vres_down, vres_mu,
           vres_v0, vres_up, k_k, k_a, r_k, lnx_g, lnx_b, pool_w, pool_scale, w_out, ln2_g, peer_q,
           peer_keys, peer_u, peer_v, lnf_g):
    B, S, D = x.shape
    L = ada_w.shape[0]
    T = B * S
    assert S % TS == 0 and S % TM == 0 and T % TQ == 0 and T % TT == 0 and TT <= S
    row = lambda t: t.reshape(1, -1)

    c_pad = jnp.zeros((8, D), F32).at[:B].set(c)
    mod = _ada_call(c_pad, ada_w, ada_b)[:, :B].reshape(L, B, 6, D)

    half = LORA_COLS // 2
    x2 = x.reshape(T, D)
    v_first = None
    for l in range(L):
        if l == 0:
            w_full = w_in[l].astype(BF16)
            vres = None
        else:
            mv = vres_down.shape[-1]
            w_full = jnp.concatenate(
                [w_in[l], vres_down[l - 1], jnp.zeros((D, VRES_PAD - mv), F32)], axis=1).astype(BF16)
            vres = (jnp.zeros((1, VRES_PAD), F32).at[:, :mv].set(vres_mu[l - 1]),
                    row(vres_v0[l - 1]),
                    jnp.zeros((VRES_PAD, RWKV_WIDTH), F32).at[:mv].set(vres_up[l - 1]),
                    v_first)
        lora_w = jnp.zeros((LORA_COLS, 2 * RWKV_WIDTH), F32)
        lora_w = lora_w.at[:half, :RWKV_WIDTH].set(w_up[l]).at[half:, RWKV_WIDTH:].set(a_up[l])
        r, k2, v, lw, kk, a, g, pool_in = _premix_call(
            x2, mod[l], row(ln1_g[l]), w_full, row(mu_shift[l]), row(w0[l]), row(a0[l]), lora_w,
            g_up[l], row(k_k[l]), row(k_a[l]), vres, S // TM)
        if l == 0:
            v_first = v
        y_rwkv = _rwkv_call(r, k2, v, lw, kk, a, g, row(r_k[l]), row(lnx_g[l]), row(lnx_b[l]), B)
        x_mid, h2t = _postmix_call(x2, y_rwkv, pool_in, mod[l], pool_w[l], row(pool_scale[l]),
                                   w_out[l].astype(BF16), row(ln2_g[l]), S // TM)
        qt_w = peer_q[l].T.astype(BF16)
        rank2, e2, bsel, fsel = _select_call(h2t, qt_w, peer_keys[l])
        x2 = _experts_call(h2t, peer_u[l].astype(BF16), peer_v[l].T.astype(BF16), rank2, e2, bsel, fsel,
                           x_mid, mod[l], row(lnf_g), S // TT, l == L - 1)
    return x2.reshape(B, S, D)
```

```python
import functools

import jax
import jax.numpy as jnp
from jax import lax
from jax.experimental import pallas as pl
from jax.experimental.pallas import tpu as pltpu

F32 = jnp.float32
BF16 = jnp.bfloat16
HIGHEST = lax.Precision.HIGHEST

HEAD_SIZE = 64
RWKV_WIDTH = 512
POOL_WIDTH = 512
POOL_GROUP = 128
POOL_WINDOWS = (2, 4, 8, 16)
MAX_WINDOW = 16
RWKV_COLS = 1792
LORA_COLS = 128
GATE_LORA = 128
VRES_PAD = 128
GN_EPS = 64e-5
NORM_EPS = 1e-6
N_KEYS = 128
PEER_HEADS = 8
PEER_TOPK = 16
CHUNK = 64
PAIR = 128
RWKV_CHUNKS_PER_ITER = 2

TM = 256
TS = 512
TQ = 512
SELECT_STRIP = 128
TT = 512
EB = 1024
FIRST_KEYS_PER_STEP = EB // N_KEYS

VMEM_LIMIT = 56 * 1024 * 1024

_CAND_ROWS = [(a, PEER_TOPK // (a + 1)) for a in range(PEER_TOPK)]
_CAND_OFFS = []
_off = 0
for _a, _nb in _CAND_ROWS:
    _CAND_OFFS.append(_off)
    _off += _nb
N_CAND = _off
N_CAND_PAD = -(-N_CAND // 8) * 8


def _dot(a, b, dims=(((1,), (0,)), ((), ())), precision=None):
    return lax.dot_general(a, b, dims, precision=precision, preferred_element_type=F32)


_NT = (((1,), (1,)), ((), ()))
_TN = (((0,), (0,)), ((), ()))
_NN = (((1,), (0,)), ((), ()))


def _dot_split(a, b, pieces, split_lhs=True):
    rem = a if split_lhs else b
    acc = None
    for i in range(pieces):
        hi = rem.astype(BF16)
        part = _dot(hi, b, _NN) if split_lhs else _dot(a, hi, _NN)
        acc = part if acc is None else acc + part
        if i + 1 < pieces:
            rem = rem - hi.astype(F32)
    return acc


def _group_ones(n, group):
    r = lax.broadcasted_iota(jnp.int32, (n, n), 0) // group
    c = lax.broadcasted_iota(jnp.int32, (n, n), 1) // group
    return (r == c).astype(F32)


def _rms_mod(x, g, shift, scale):
    y = x * lax.rsqrt(jnp.mean(x * x, axis=-1, keepdims=True) + NORM_EPS)
    return (y * g) * (1.0 + scale) + shift


def _ada_kernel(c_ref, w_ref, b_ref, o_ref):
    c = c_ref[...]
    ca = c * jax.nn.sigmoid(c)
    o_ref[0] = _dot(ca, w_ref[0], precision=HIGHEST) + b_ref[0]


def _ada_call(c_pad, ada_w, ada_b):
    L, D, D6 = ada_w.shape
    nb = D6 // D
    return pl.pallas_call(
        _ada_kernel,
        grid=(L, nb),
        in_specs=[
            pl.BlockSpec((8, D), lambda l, j: (0, 0)),
            pl.BlockSpec((1, D, D), lambda l, j: (l, 0, j)),
            pl.BlockSpec((1, 1, D), lambda l, j: (l, 0, j)),
        ],
        out_specs=pl.BlockSpec((1, 8, D), lambda l, j: (l, 0, j)),
        out_shape=jax.ShapeDtypeStruct((L, 8, D6), F32),
        name="adaln",
    )(c_pad, ada_w, ada_b.reshape(L, 1, D6))


def _premix_kernel(*refs, tiles_per_batch, has_vres):
    if has_vres:
        (x_ref, mod_ref, ln_ref, w_ref, mu_ref, w0_ref, a0_ref, lora_ref, gup_ref, kk_ref, ka_ref,
         vmu_ref, v0_ref, vup_ref, vfirst_ref,
         r_out, k_out, v_out, lw_out, kkn_out, a_out, g_out, p_out, carry_ref) = refs
    else:
        (x_ref, mod_ref, ln_ref, w_ref, mu_ref, w0_ref, a0_ref, lora_ref, gup_ref, kk_ref, ka_ref,
         r_out, k_out, v_out, lw_out, kkn_out, a_out, g_out, p_out, carry_ref) = refs
    i = pl.program_id(0)

    @pl.when(i % tiles_per_batch == 0)
    def _():
        carry_ref[...] = jnp.zeros_like(carry_ref)

    m = mod_ref[0]
    h = _rms_mod(x_ref[...], ln_ref[...], m[0:1], m[1:2])
    proj = _dot(h.astype(BF16), w_ref[...])
    tm = proj.shape[0]
    row0 = lax.broadcasted_iota(jnp.int32, (tm, 1), 0) == 0
    carry = carry_ref[...]
    carry_ref[...] = proj[tm - 1:tm, :]

    def shifted(lo, hi, mu):
        z = proj[:, lo:hi]
        prev = jnp.where(row0, carry[:, lo:hi], pltpu.roll(z, 1, 0))
        return z + (prev - z) * mu

    rw = shifted(0, RWKV_COLS, mu_ref[...])
    r = rw[:, 0:RWKV_WIDTH]
    k = rw[:, RWKV_WIDTH:2 * RWKV_WIDTH]
    v = rw[:, 2 * RWKV_WIDTH:3 * RWKV_WIDTH]
    lo = 3 * RWKV_WIDTH
    wa = rw[:, lo:lo + LORA_COLS]
    gd = rw[:, lo + LORA_COLS:lo + LORA_COLS + GATE_LORA]
    lane = lax.broadcasted_iota(jnp.int32, wa.shape, 1)
    wa = jnp.where(lane < LORA_COLS // 2, jnp.tanh(wa), wa)
    lora = _dot(wa, lora_ref[...], precision=HIGHEST)
    zw = -(w0_ref[...] + lora[:, 0:RWKV_WIDTH])
    softplus = jnp.maximum(zw, 0.0) + jnp.log1p(jnp.exp(-jnp.abs(zw)))
    w = -softplus - 0.5
    a = jax.nn.sigmoid(a0_ref[...] + lora[:, RWKV_WIDTH:2 * RWKV_WIDTH])
    g = _dot(jax.nn.sigmoid(gd), gup_ref[...], precision=HIGHEST)
    kk = k * kk_ref[...]
    ss = _dot(kk * kk, _group_ones(RWKV_WIDTH, HEAD_SIZE), precision=HIGHEST)
    kk = kk / jnp.maximum(jnp.sqrt(ss), 1e-12)
    k2 = k * (1.0 + (a - 1.0) * ka_ref[...])
    if has_vres:
        base = RWKV_COLS + POOL_WIDTH
        vd = shifted(base, base + VRES_PAD, vmu_ref[...])
        mixv = jax.nn.sigmoid(v0_ref[...] + _dot(vd, vup_ref[...], precision=HIGHEST))
        v = v + (vfirst_ref[...] - v) * mixv
    r_out[...] = r
    k_out[...] = k2
    v_out[...] = v
    lw_out[...] = -jnp.exp(w)
    kkn_out[...] = kk
    a_out[...] = a
    g_out[...] = g
    p_out[...] = proj[:, RWKV_COLS:RWKV_COLS + POOL_WIDTH]


def _premix_call(x2, modl, ln_g, w_full, mu, w0, a0, lora_w, g_up, k_k, k_a, vres, tiles_per_batch):
    T, D = x2.shape
    NC = w_full.shape[1]
    has_vres = vres is not None
    row = lambda n: pl.BlockSpec((1, n), lambda i: (0, 0))
    full = lambda a: pl.BlockSpec(a.shape, lambda i: (0,) * a.ndim)
    tok = lambda n: pl.BlockSpec((TM, n), lambda i: (i, 0))
    in_specs = [tok(D), pl.BlockSpec((1, 6, D), lambda i: (i // tiles_per_batch, 0, 0)), row(D),
                full(w_full), row(RWKV_COLS), row(RWKV_WIDTH), row(RWKV_WIDTH), full(lora_w), full(g_up),
                row(RWKV_WIDTH), row(RWKV_WIDTH)]
    args = [x2, modl, ln_g, w_full, mu, w0, a0, lora_w, g_up, k_k, k_a]
    if has_vres:
        vmu, v0, vup, vfirst = vres
        in_specs += [row(VRES_PAD), row(RWKV_WIDTH), full(vup), tok(RWKV_WIDTH)]
        args += [vmu, v0, vup, vfirst]
    out = jax.ShapeDtypeStruct((T, RWKV_WIDTH), F32)
    return pl.pallas_call(
        functools.partial(_premix_kernel, tiles_per_batch=tiles_per_batch, has_vres=has_vres),
        grid=(T // TM,),
        in_specs=in_specs,
        out_specs=[tok(RWKV_WIDTH)] * 8,
        out_shape=[out] * 8,
        scratch_shapes=[pltpu.VMEM((1, NC), F32)],
        compiler_params=pltpu.CompilerParams(dimension_semantics=("arbitrary",),
                                             vmem_limit_bytes=VMEM_LIMIT),
        name="premix",
    )(*args)


def _rwkv_kernel(r_ref, k_ref, v_ref, lw_ref, kk_ref, a_ref, g_ref, rk_ref, lng_ref, lnb_ref,
                 o_ref, state_ref, y_ref, ra_ref, yu_ref, xbk_ref, xv_ref, gc_ref):
    @pl.when(pl.program_id(1) == 0)
    def _():
        state_ref[...] = jnp.zeros_like(state_ref)

    ts = r_ref.shape[0]
    n_pairs = r_ref.shape[1] // PAIR
    n_chunks = ts // CHUNK
    ti = lax.broadcasted_iota(jnp.int32, (CHUNK, CHUNK), 0)
    si = lax.broadcasted_iota(jnp.int32, (CHUNK, CHUNK), 1)
    tril = (ti >= si).astype(BF16)
    ri = lax.broadcasted_iota(jnp.int32, (2 * PAIR, 2 * PAIR), 0)
    ci = lax.broadcasted_iota(jnp.int32, (2 * PAIR, 2 * PAIR), 1)
    keep = (ri % CHUNK) + (ri >= PAIR).astype(jnp.int32) > (ci % CHUNK)
    ei = lax.broadcasted_iota(jnp.int32, (PAIR, PAIR), 0)
    ej = lax.broadcasted_iota(jnp.int32, (PAIR, PAIR), 1)
    eye = (ei == ej).astype(F32)
    lane = lax.broadcasted_iota(jnp.int32, (CHUNK, PAIR), 1)
    head0 = lane < HEAD_SIZE
    zeros_pp = jnp.zeros((PAIR, PAIR), BF16)

    def stack(t):
        return jnp.concatenate([jnp.where(head0, t, 0.0), jnp.where(head0, 0.0, t)], axis=0).astype(BF16)

    def pre_body(it, carry):
        chains = []
        for cc in range(RWKV_CHUNKS_PER_ITER):
            c = it * RWKV_CHUNKS_PER_ITER + cc
            rows = pl.ds(pl.multiple_of(c * CHUNK, CHUNK), CHUNK)
            lw = lw_ref[rows, :]
            cum = _dot_split(tril, lw, 3, split_lhs=False)
            g_in = jnp.exp(cum)
            g_ex = jnp.exp(cum - lw)
            g_inv = jnp.exp(-cum)
            gc_ref[c] = g_in[CHUNK - 1:CHUNK, :]
            kk = kk_ref[rows, :]
            rt = r_ref[rows, :] * g_in
            kt = k_ref[rows, :] * g_inv
            bt = (kk * a_ref[rows, :]) * g_inv
            at = -kk * g_ex
            vv = v_ref[rows, :]
            for p in range(n_pairs):
                ls = slice(p * PAIR, (p + 1) * PAIR)
                xa, xb, xk, xr, xv = (stack(t[:, ls]) for t in (at, bt, kt, rt, vv))
                xbk = jnp.concatenate([xb, xk], axis=0)
                xbk_ref[c, p] = xbk
                xv_ref[c, p] = xv
                chains.append(dict(c=c, p=p, xa=xa, xr=xr, xv=xv, xbk=xbk))
        for ch in chains:
            big = _dot(jnp.concatenate([ch["xa"], ch["xr"]], axis=0), ch["xbk"], _NT)
            big = jnp.where(keep, big, 0.0)
            ch["lab"] = big[0:PAIR, 0:PAIR]
            ch["lak"] = big[0:PAIR, PAIR:2 * PAIR].astype(BF16)
            ch["lrbk"] = big[PAIR:2 * PAIR, :].astype(BF16)
        for ch in chains:
            lb = ch["lab"].astype(BF16)
            ch["pw"] = _dot(lb, lb)
            ch["lakv"] = _dot(ch["lak"], ch["xv"]).astype(BF16)
            ch["inv"] = eye + ch["lab"]
        for i in range(1, 6):
            for ch in chains:
                pb = ch["pw"].astype(BF16)
                sb = ch["inv"].astype(BF16)
                if i < 5:
                    res = _dot(pb, jnp.concatenate([pb, sb], axis=1))
                    ch["pw"] = res[:, 0:PAIR]
                    ch["inv"] = ch["inv"] + res[:, PAIR:2 * PAIR]
                else:
                    ch["inv"] = ch["inv"] + _dot(pb, sb)
        for ch in chains:
            au = _dot(ch["inv"].astype(BF16), jnp.concatenate([ch["xa"], ch["lakv"]], axis=1))
            ch["a2"] = au[:, 0:PAIR].astype(BF16)
            ch["uv"] = au[:, PAIR:2 * PAIR]
        for ch in chains:
            rhs = jnp.concatenate(
                [jnp.concatenate([ch["a2"], ch["uv"].astype(BF16)], axis=1),
                 jnp.concatenate([zeros_pp, ch["xv"]], axis=1)], axis=0)
            ry = _dot(ch["lrbk"], rhs)
            r2 = ch["xr"].astype(F32) + ry[:, 0:PAIR]
            ra_ref[ch["c"], ch["p"]] = jnp.concatenate([r2.astype(BF16), ch["a2"]], axis=0)
            yu_ref[ch["c"], ch["p"]] = jnp.concatenate([ry[:, PAIR:2 * PAIR], ch["uv"]], axis=0)
        return carry

    lax.fori_loop(0, n_chunks // RWKV_CHUNKS_PER_ITER, pre_body, 0)

    def seq_body(c, carry):
        rows = pl.ds(pl.multiple_of(c * CHUNK, CHUNK), CHUNK)
        gc = gc_ref[c]
        res = [_dot(ra_ref[c, p], state_ref[p].astype(BF16), _NT) + yu_ref[c, p] for p in range(n_pairs)]
        for p in range(n_pairs):
            ls = slice(p * PAIR, (p + 1) * PAIR)
            y_st = res[p][0:PAIR]
            u_st = res[p][PAIR:2 * PAIR].astype(BF16)
            st = _dot(jnp.concatenate([u_st, xv_ref[c, p]], axis=0), xbk_ref[c, p], _TN)
            state_ref[p] = (state_ref[p] + st) * gc[:, ls]
            y_ref[rows, ls] = y_st[0:CHUNK] + y_st[CHUNK:2 * CHUNK]
        return carry

    lax.fori_loop(0, n_chunks, seq_body, 0)

    ones = _group_ones(RWKV_WIDTH, HEAD_SIZE).astype(BF16)
    y = y_ref[...]
    mean = _dot_split(y, ones, 2) * (1.0 / HEAD_SIZE)
    d = y - mean
    var = _dot_split(d * d, ones, 2) * (1.0 / HEAD_SIZE)
    yn = d * lax.rsqrt(var + GN_EPS) * lng_ref[...] + lnb_ref[...]
    bonus = _dot_split(r_ref[...] * k_ref[...] * rk_ref[...], ones, 2) * v_ref[...]
    o_ref[...] = (yn + bonus) * g_ref[...]


def _rwkv_call(r, k, v, lw, kk, a, g, r_k, lnx_g, lnx_b, batch):
    T, W = r.shape
    steps = T // batch // TS
    n_chunks, n_pairs = TS // CHUNK, W // PAIR
    tok = pl.BlockSpec((TS, W), lambda b, j: (b * steps + j, 0))
    row = pl.BlockSpec((1, W), lambda b, j: (0, 0))
    return pl.pallas_call(
        _rwkv_kernel,
        grid=(batch, steps),
        in_specs=[tok] * 7 + [row] * 3,
        out_specs=tok,
        out_shape=jax.ShapeDtypeStruct((T, W), F32),
        scratch_shapes=[pltpu.VMEM((n_pairs, PAIR, PAIR), F32), pltpu.VMEM((TS, W), F32),
                        pltpu.VMEM((n_chunks, n_pairs, 2 * PAIR, PAIR), BF16),
                        pltpu.VMEM((n_chunks, n_pairs, 2 * PAIR, PAIR), F32),
                        pltpu.VMEM((n_chunks, n_pairs, 2 * PAIR, PAIR), BF16),
                        pltpu.VMEM((n_chunks, n_pairs, PAIR, PAIR), BF16),
                        pltpu.VMEM((n_chunks, 1, W), F32)],
        compiler_params=pltpu.CompilerParams(dimension_semantics=("arbitrary", "arbitrary"),
                                             vmem_limit_bytes=VMEM_LIMIT),
        name="rwkv",
    )(r, k, v, lw, kk, a, g, r_k, lnx_g, lnx_b)


def _postmix_kernel(x_ref, y_ref, p_ref, mod_ref, pw_ref, ps_ref, wo_ref, ln_ref,
                    x_out, ht_out, carry_ref, *, tiles_per_batch):
    i = pl.program_id(0)
    j = i % tiles_per_batch

    @pl.when(j == 0)
    def _():
        carry_ref[...] = jnp.zeros_like(carry_ref)

    m = mod_ref[0]
    p = p_ref[...]
    tm = p.shape[0]
    ext = jnp.concatenate([carry_ref[...], p], axis=0)
    carry_ref[...] = p[tm - MAX_WINDOW:tm, :]
    pos = (j * tm + 1 + lax.broadcasted_iota(jnp.int32, (tm, 1), 0)).astype(F32)
    pooled = []
    for gi, win in enumerate(POOL_WINDOWS):
        ls = slice(gi * POOL_GROUP, (gi + 1) * POOL_GROUP)
        wsum = ext[:, ls]
        span = 1
        while span < win:
            wsum = wsum + pltpu.roll(wsum, span, 0)
            span *= 2
        mean = wsum[MAX_WINDOW:, :] / jnp.minimum(pos, float(win))
        d = mean - p[:, ls]
        pooled.append(_dot(d, pw_ref[gi], precision=HIGHEST))
    y_pool = jnp.concatenate(pooled, axis=1) * ps_ref[...]
    mix = (_dot(y_ref[...].astype(BF16), wo_ref[0:RWKV_WIDTH, :])
           + _dot(y_pool.astype(BF16), wo_ref[RWKV_WIDTH:RWKV_WIDTH + POOL_WIDTH, :]))
    x = x_ref[...] + m[2:3] * mix
    x_out[...] = x
    h2 = _rms_mod(x, ln_ref[...], m[3:4], m[4:5])
    ht_out[...] = h2.T.astype(BF16)


def _postmix_call(x2, y_rwkv, pool_in, modl, pool_w, pool_scale, w_out, ln_g, tiles_per_batch):
    T, D = x2.shape
    row = lambda n: pl.BlockSpec((1, n), lambda i: (0, 0))
    full = lambda a: pl.BlockSpec(a.shape, lambda i: (0,) * a.ndim)
    tok = lambda n: pl.BlockSpec((TM, n), lambda i: (i, 0))
    return pl.pallas_call(
        functools.partial(_postmix_kernel, tiles_per_batch=tiles_per_batch),
        grid=(T // TM,),
        in_specs=[tok(D), tok(RWKV_WIDTH), tok(POOL_WIDTH),
                  pl.BlockSpec((1, 6, D), lambda i: (i // tiles_per_batch, 0, 0)),
                  full(pool_w), row(POOL_WIDTH), full(w_out), row(D)],
        out_specs=[tok(D), pl.BlockSpec((D, TM), lambda i: (0, i))],
        out_shape=[jax.ShapeDtypeStruct((T, D), F32), jax.ShapeDtypeStruct((D, T), BF16)],
        scratch_shapes=[pltpu.VMEM((MAX_WINDOW, POOL_WIDTH), F32)],
        compiler_params=pltpu.CompilerParams(dimension_semantics=("arbitrary",),
                                             vmem_limit_bytes=VMEM_LIMIT),
        name="postmix",
    )(x2, y_rwkv, pool_in, modl, pool_w, pool_scale, w_out, ln_g)


def _extract16(cur, mark, marks, tie_break):
    n, w = cur.shape
    iota = lax.broadcasted_iota(jnp.int32, (n, w), 0).astype(F32)
    vals = []
    for j in range(PEER_TOPK):
        mx = jnp.max(cur, axis=0, keepdims=True)
        hit = cur == mx
        if tie_break:
            first = jnp.min(jnp.where(hit, iota, float(n)), axis=0, keepdims=True)
            hit = iota == first
        mark = jnp.where(hit, marks[j], mark)
        cur = jnp.where(hit, -jnp.inf, cur)
        vals.append(mx)
    removed = jnp.sum((cur == -jnp.inf).astype(F32), axis=0, keepdims=True)
    return mark, vals, removed


def _select_strip(s1, s2, outs, lanes, v1_ref, v2_ref, cand_ref, bm_ref, tie_break):
    rank2_out, e2_out, b_out, f_out = outs
    ranks = [float(j + 1) for j in range(PEER_TOPK)]
    unranked = jnp.full(s1.shape, 99.0, F32)
    rank1, vals1, removed1 = _extract16(s1, unranked, ranks, tie_break)
    rank2, vals2, removed2 = _extract16(s2, unranked, ranks, tie_break)
    for j in range(PEER_TOPK):
        v1_ref[j:j + 1, :] = vals1[j]
        v2_ref[j:j + 1, :] = vals2[j]
    w = s1.shape[1]
    for (a, nb), off in zip(_CAND_ROWS, _CAND_OFFS):
        cand_ref[off:off + nb, :] = vals1[a] + v2_ref[0:nb, :]
    cand_ref[N_CAND:N_CAND_PAD, :] = jnp.full((N_CAND_PAD - N_CAND, w), -jnp.inf, F32)
    cand = cand_ref[...]
    chosen, _, removed3 = _extract16(cand, jnp.zeros(cand.shape, F32), [1.0] * PEER_TOPK, tie_break)
    z = jnp.sum(chosen * jnp.exp(cand - cand[0:1, :]), axis=0, keepdims=True)
    cand_ref[...] = chosen
    for (a, nb), off in zip(_CAND_ROWS, _CAND_OFFS):
        bm_ref[a:a + 1, :] = jnp.sum(cand_ref[off:off + nb, :], axis=0, keepdims=True)
    bsel = jnp.zeros(s1.shape, F32)
    for a in range(PEER_TOPK):
        bsel = jnp.where(rank1 == ranks[a], bm_ref[a:a + 1, :], bsel)
    rank2_out[0, :, lanes] = rank2.astype(BF16)
    e2_out[0, :, lanes] = jnp.exp(s2 - vals2[0]).astype(BF16)
    b_out[0, :, lanes] = bsel
    f_out[0, :, lanes] = jnp.exp(s1 - vals1[0]) / z
    removed3 = removed3 - float(N_CAND_PAD - N_CAND)
    return jnp.max(jnp.maximum(jnp.maximum(removed1, removed2), removed3))


def _select_kernel(ht_ref, qt_ref, keys_ref, rank2_out, e2_out, b_out, f_out,
                   s1_ref, s2_ref, v1_ref, v2_ref, cand_ref, bm_ref):
    qt = _dot(qt_ref[...], ht_ref[...])
    s1_ref[...] = _dot(keys_ref[0, 0], qt[0:N_KEYS], precision=HIGHEST)
    s2_ref[...] = _dot(keys_ref[0, 1], qt[N_KEYS:2 * N_KEYS], precision=HIGHEST)
    outs = (rank2_out, e2_out, b_out, f_out)

    def strip(j, carry):
        lanes = pl.ds(pl.multiple_of(j * SELECT_STRIP, SELECT_STRIP), SELECT_STRIP)
        s1 = s1_ref[:, lanes]
        s2 = s2_ref[:, lanes]
        most_removed = _select_strip(s1, s2, outs, lanes, v1_ref, v2_ref, cand_ref, bm_ref, False)

        @pl.when(most_removed > float(PEER_TOPK))
        def _():
            _select_strip(s1, s2, outs, lanes, v1_ref, v2_ref, cand_ref, bm_ref, True)

        return carry

    lax.fori_loop(0, s1_ref.shape[1] // SELECT_STRIP, strip, 0)


def _select_call(h2t, qt_w, keys):
    D, T = h2t.shape
    out = jax.ShapeDtypeStruct((PEER_HEADS, N_KEYS, T), F32)
    out16 = jax.ShapeDtypeStruct((PEER_HEADS, N_KEYS, T), BF16)
    ospec = pl.BlockSpec((1, N_KEYS, TQ), lambda i, h: (h, 0, i))
    return pl.pallas_call(
        _select_kernel,
        grid=(T // TQ, PEER_HEADS),
        in_specs=[pl.BlockSpec((D, TQ), lambda i, h: (0, i)),
                  pl.BlockSpec((2 * N_KEYS, D), lambda i, h: (h, 0)),
                  pl.BlockSpec((1, 2, N_KEYS, N_KEYS), lambda i, h: (h, 0, 0, 0))],
        out_specs=[ospec] * 4,
        out_shape=[out16, out16, out, out],
        scratch_shapes=[pltpu.VMEM((N_KEYS, TQ), F32), pltpu.VMEM((N_KEYS, TQ), F32),
                        pltpu.VMEM((PEER_TOPK, SELECT_STRIP), F32), pltpu.VMEM((PEER_TOPK, SELECT_STRIP), F32),
                        pltpu.VMEM((N_CAND_PAD, SELECT_STRIP), F32),
                        pltpu.VMEM((PEER_TOPK, SELECT_STRIP), F32)],
        compiler_params=pltpu.CompilerParams(dimension_semantics=("arbitrary", "arbitrary"),
                                             vmem_limit_bytes=VMEM_LIMIT),
        name="select",
    )(h2t, qt_w, keys)


def _experts_kernel(ht_ref, u_ref, vt_ref, rank2_ref, e2_ref, b_ref, f_ref, x_ref, mod_ref, lnf_ref,
                    x_out, acc_ref, *, tiles_per_batch, final):
    e = pl.program_id(1)

    @pl.when(e == 0)
    def _():
        acc_ref[...] = jnp.zeros_like(acc_ref)

    act = _dot(u_ref[...], ht_ref[...])
    gelu = (0.5 * act * (1.0 + lax.erf(act * (2.0 ** -0.5)))).astype(BF16)
    tt = act.shape[1]
    zs = []
    for ii in range(FIRST_KEYS_PER_STEP):
        gate = None
        for h in range(PEER_HEADS):
            brow = jnp.broadcast_to(b_ref[h, e, ii:ii + 1, :], (N_KEYS, tt)).astype(BF16)
            frow = jnp.broadcast_to(f_ref[h, e, ii:ii + 1, :], (N_KEYS, tt)).astype(BF16)
            term = jnp.where(rank2_ref[h] <= brow, e2_ref[h], 0.0) * frow
            gate = term if gate is None else gate + term
        zs.append(gate * gelu[ii * N_KEYS:(ii + 1) * N_KEYS])
    z = jnp.concatenate(zs, axis=0)
    acc_ref[...] += _dot(vt_ref[...], z)

    @pl.when(e == pl.num_programs(1) - 1)
    def _():
        m = mod_ref[0]
        x = x_ref[...] + m[5:6] * acc_ref[...].T
        if final:
            x = x * lax.rsqrt(jnp.mean(x * x, axis=-1, keepdims=True) + NORM_EPS) * lnf_ref[...]
        x_out[...] = x


def _experts_call(h2t, u_bf, vt_bf, rank2, e2, bsel, fsel, x2, modl, lnf_g, tiles_per_batch, final):
    D, T = h2t.shape
    NE = u_bf.shape[0]
    sel = pl.BlockSpec((PEER_HEADS, N_KEYS, TT), lambda i, e: (0, 0, i))
    bsel, fsel = (t.reshape(PEER_HEADS, N_KEYS // FIRST_KEYS_PER_STEP, FIRST_KEYS_PER_STEP, T)
                  for t in (bsel, fsel))
    sel1 = pl.BlockSpec(bsel.shape[:3] + (TT,), lambda i, e: (0, 0, 0, i))
    tok = pl.BlockSpec((TT, D), lambda i, e: (i, 0))
    return pl.pallas_call(
        functools.partial(_experts_kernel, tiles_per_batch=tiles_per_batch, final=final),
        grid=(T // TT, NE // EB),
        in_specs=[pl.BlockSpec((D, TT), lambda i, e: (0, i)),
                  pl.BlockSpec((EB, D), lambda i, e: (e, 0)),
                  pl.BlockSpec((D, EB), lambda i, e: (0, e)),
                  sel, sel, sel1, sel1, tok,
                  pl.BlockSpec((1, 6, D), lambda i, e: (i // tiles_per_batch, 0, 0)),
                  pl.BlockSpec((1, D), lambda i, e: (0, 0))],
        out_specs=tok,
        out_shape=jax.ShapeDtypeStruct((T, D), F32),
        scratch_shapes=[pltpu.VMEM((D, TT), F32)],
        compiler_params=pltpu.CompilerParams(dimension_semantics=("arbitrary", "arbitrary"),
                                             vmem_limit_bytes=VMEM_LIMIT),
        name="experts",
    )(h2t, u_bf, vt_bf, rank2, e2, bsel, fsel, x2, modl, lnf_g)


def kernel(x, c, ada_w, ada_b, ln1_g, w_in, mu_shift, w0, w_up, a0, a_up, g_up, vres_down, vres_mu,
           vres_v0, vres_up, k_k, k_a, r_k, lnx_g, lnx_b, pool_w, pool_scale, w_out, ln2_g, peer_q,
           peer_keys, peer_u, peer_v, lnf_g):
    B, S, D = x.shape
    L = ada_w.shape[0]
    T = B * S
    assert S % TS == 0 and S % TM == 0 and T % TQ == 0 and T % TT == 0 and TT <= S
    row = lambda t: t.reshape(1, -1)

    c_pad = jnp.zeros((8, D), F32).at[:B].set(c)
    mod = _ada_call(c_pad, ada_w, ada_b)[:, :B].reshape(L, B, 6, D)

    half = LORA_COLS // 2
    x2 = x.reshape(T, D)
    v_first = None
    for l in range(L):
        if l == 0:
            w_full = w_in[l].astype(BF16)
            vres = None
        else:
            mv = vres_down.shape[-1]
            w_full = jnp.concatenate(
                [w_in[l], vres_down[l - 1], jnp.zeros((D, VRES_PAD - mv), F32)], axis=1).astype(BF16)
            vres = (jnp.zeros((1, VRES_PAD), F32).at[:, :mv].set(vres_mu[l - 1]),
                    row(vres_v0[l - 1]),
                    jnp.zeros((VRES_PAD, RWKV_WIDTH), F32).at[:mv].set(vres_up[l - 1]),
                    v_first)
        lora_w = jnp.zeros((LORA_COLS, 2 * RWKV_WIDTH), F32)
        lora_w = lora_w.at[:half, :RWKV_WIDTH].set(w_up[l]).at[half:, RWKV_WIDTH:].set(a_up[l])
        r, k2, v, lw, kk, a, g, pool_in = _premix_call(
            x2, mod[l], row(ln1_g[l]), w_full, row(mu_shift[l]), row(w0[l]), row(a0[l]), lora_w,
            g_up[l], row(k_k[l]), row(k_a[l]), vres, S // TM)
        if l == 0:
            v_first = v
        y_rwkv = _rwkv_call(r, k2, v, lw, kk, a, g, row(r_k[l]), row(lnx_g[l]), row(lnx_b[l]), B)
        x_mid, h2t = _postmix_call(x2, y_rwkv, pool_in, mod[l], pool_w[l], row(pool_scale[l]),
                                   w_out[l].astype(BF16), row(ln2_g[l]), S // TM)
        qt_w = peer_q[l].T.astype(BF16)
        rank2, e2, bsel, fsel = _select_call(h2t, qt_w, peer_keys[l])
        x2 = _experts_call(h2t, peer_u[l].astype(BF16), peer_v[l].T.astype(BF16), rank2, e2, bsel, fsel,
                           x_mid, mod[l], row(lnf_g), S // TT, l == L - 1)
    return x2.reshape(B, S, D)
```

```python
import functools

import jax
import jax.numpy as jnp
from jax import lax
from jax.experimental import pallas as pl
from jax.experimental.pallas import tpu as pltpu

F32 = jnp.float32
BF16 = jnp.bfloat16
HIGHEST = lax.Precision.HIGHEST

HEAD_SIZE = 64
RWKV_WIDTH = 512
POOL_WIDTH = 512
POOL_GROUP = 128
POOL_WINDOWS = (2, 4, 8, 16)
MAX_WINDOW = 16
RWKV_COLS = 1792
LORA_COLS = 128
GATE_LORA = 128
VRES_PAD = 128
GN_EPS = 64e-5
NORM_EPS = 1e-6
N_KEYS = 128
PEER_HEADS = 8
PEER_TOPK = 16
CHUNK = 64
PAIR = 128
RWKV_CHUNKS_PER_ITER = 2

TM = 256
TS = 512
TQ = 512
SELECT_STRIP = 128
TT = 1024
EXPERT_SUB = 1024
EB = 1024
FIRST_KEYS_PER_STEP = EB // N_KEYS

VMEM_LIMIT = 56 * 1024 * 1024

_CAND_ROWS = [(a, PEER_TOPK // (a + 1)) for a in range(PEER_TOPK)]
_CAND_OFFS = []
_off = 0
for _a, _nb in _CAND_ROWS:
    _CAND_OFFS.append(_off)
    _off += _nb
N_CAND = _off
N_CAND_PAD = -(-N_CAND // 8) * 8


def _dot(a, b, dims=(((1,), (0,)), ((), ())), precision=None):
    return lax.dot_general(a, b, dims, precision=precision, preferred_element_type=F32)


_NT = (((1,), (1,)), ((), ()))
_TN = (((0,), (0,)), ((), ()))
_NN = (((1,), (0,)), ((), ()))


def _dot_split(a, b, pieces, split_lhs=True):
    rem = a if split_lhs else b
    acc = None
    for i in range(pieces):
        hi = rem.astype(BF16)
        part = _dot(hi, b, _NN) if split_lhs else _dot(a, hi, _NN)
        acc = part if acc is None else acc + part
        if i + 1 < pieces:
            rem = rem - hi.astype(F32)
    return acc


def _group_ones(n, group):
    r = lax.broadcasted_iota(jnp.int32, (n, n), 0) // group
    c = lax.broadcasted_iota(jnp.int32, (n, n), 1) // group
    return (r == c).astype(F32)


def _rms_mod(x, g, shift, scale):
    y = x * lax.rsqrt(jnp.mean(x * x, axis=-1, keepdims=True) + NORM_EPS)
    return (y * g) * (1.0 + scale) + shift


def _ada_kernel(c_ref, w_ref, b_ref, o_ref):
    c = c_ref[...]
    ca = c * jax.nn.sigmoid(c)
    o_ref[0] = _dot(ca, w_ref[0], precision=HIGHEST) + b_ref[0]


def _ada_call(c_pad, ada_w, ada_b):
    L, D, D6 = ada_w.shape
    nb = D6 // D
    return pl.pallas_call(
        _ada_kernel,
        grid=(L, nb),
        in_specs=[
            pl.BlockSpec((8, D), lambda l, j: (0, 0)),
            pl.BlockSpec((1, D, D), lambda l, j: (l, 0, j)),
            pl.BlockSpec((1, 1, D), lambda l, j: (l, 0, j)),
        ],
        out_specs=pl.BlockSpec((1, 8, D), lambda l, j: (l, 0, j)),
        out_shape=jax.ShapeDtypeStruct((L, 8, D6), F32),
        name="adaln",
    )(c_pad, ada_w, ada_b.reshape(L, 1, D6))


def _premix_kernel(*refs, tiles_per_batch, has_vres):
    if has_vres:
        (x_ref, mod_ref, ln_ref, w_ref, mu_ref, w0_ref, a0_ref, lora_ref, gup_ref, kk_ref, ka_ref,
         vmu_ref, v0_ref, vup_ref, vfirst_ref,
         r_out, k_out, v_out, lw_out, kkn_out, a_out, g_out, p_out, carry_ref) = refs
    else:
        (x_ref, mod_ref, ln_ref, w_ref, mu_ref, w0_ref, a0_ref, lora_ref, gup_ref, kk_ref, ka_ref,
         r_out, k_out, v_out, lw_out, kkn_out, a_out, g_out, p_out, carry_ref) = refs
    i = pl.program_id(0)

    @pl.when(i % tiles_per_batch == 0)
    def _():
        carry_ref[...] = jnp.zeros_like(carry_ref)

    m = mod_ref[0]
    h = _rms_mod(x_ref[...], ln_ref[...], m[0:1], m[1:2])
    proj = _dot(h.astype(BF16), w_ref[...])
    tm = proj.shape[0]
    row0 = lax.broadcasted_iota(jnp.int32, (tm, 1), 0) == 0
    carry = carry_ref[...]
    carry_ref[...] = proj[tm - 1:tm, :]

    def shifted(lo, hi, mu):
        z = proj[:, lo:hi]
        prev = jnp.where(row0, carry[:, lo:hi], pltpu.roll(z, 1, 0))
        return z + (prev - z) * mu

    rw = shifted(0, RWKV_COLS, mu_ref[...])
    r = rw[:, 0:RWKV_WIDTH]
    k = rw[:, RWKV_WIDTH:2 * RWKV_WIDTH]
    v = rw[:, 2 * RWKV_WIDTH:3 * RWKV_WIDTH]
    lo = 3 * RWKV_WIDTH
    wa = rw[:, lo:lo + LORA_COLS]
    gd = rw[:, lo + LORA_COLS:lo + LORA_COLS + GATE_LORA]
    lane = lax.broadcasted_iota(jnp.int32, wa.shape, 1)
    wa = jnp.where(lane < LORA_COLS // 2, jnp.tanh(wa), wa)
    lora = _dot(wa, lora_ref[...], precision=HIGHEST)
    zw = -(w0_ref[...] + lora[:, 0:RWKV_WIDTH])
    softplus = jnp.maximum(zw, 0.0) + jnp.log1p(jnp.exp(-jnp.abs(zw)))
    w = -softplus - 0.5
    a = jax.nn.sigmoid(a0_ref[...] + lora[:, RWKV_WIDTH:2 * RWKV_WIDTH])
    g = _dot(jax.nn.sigmoid(gd), gup_ref[...], precision=HIGHEST)
    kk = k * kk_ref[...]
    ss = _dot(kk * kk, _group_ones(RWKV_WIDTH, HEAD_SIZE), precision=HIGHEST)
    kk = kk / jnp.maximum(jnp.sqrt(ss), 1e-12)
    k2 = k * (1.0 + (a - 1.0) * ka_ref[...])
    if has_vres:
        base = RWKV_COLS + POOL_WIDTH
        vd = shifted(base, base + VRES_PAD, vmu_ref[...])
        mixv = jax.nn.sigmoid(v0_ref[...] + _dot(vd, vup_ref[...], precision=HIGHEST))
        v = v + (vfirst_ref[...] - v) * mixv
    r_out[...] = r
    k_out[...] = k2
    v_out[...] = v
    lw_out[...] = -jnp.exp(w)
    kkn_out[...] = kk
    a_out[...] = a
    g_out[...] = g
    p_out[...] = proj[:, RWKV_COLS:RWKV_COLS + POOL_WIDTH]


def _premix_call(x2, modl, ln_g, w_full, mu, w0, a0, lora_w, g_up, k_k, k_a, vres, tiles_per_batch):
    T, D = x2.shape
    NC = w_full.shape[1]
    has_vres = vres is not None
    row = lambda n: pl.BlockSpec((1, n), lambda i: (0, 0))
    full = lambda a: pl.BlockSpec(a.shape, lambda i: (0,) * a.ndim)
    tok = lambda n: pl.BlockSpec((TM, n), lambda i: (i, 0))
    in_specs = [tok(D), pl.BlockSpec((1, 6, D), lambda i: (i // tiles_per_batch, 0, 0)), row(D),
                full(w_full), row(RWKV_COLS), row(RWKV_WIDTH), row(RWKV_WIDTH), full(lora_w), full(g_up),
                row(RWKV_WIDTH), row(RWKV_WIDTH)]
    args = [x2, modl, ln_g, w_full, mu, w0, a0, lora_w, g_up, k_k, k_a]
    if has_vres:
        vmu, v0, vup, vfirst = vres
        in_specs += [row(VRES_PAD), row(RWKV_WIDTH), full(vup), tok(RWKV_WIDTH)]
        args += [vmu, v0, vup, vfirst]
    out = jax.ShapeDtypeStruct((T, RWKV_WIDTH), F32)
    return pl.pallas_call(
        functools.partial(_premix_kernel, tiles_per_batch=tiles_per_batch, has_vres=has_vres),
        grid=(T // TM,),
        in_specs=in_specs,
        out_specs=[tok(RWKV_WIDTH)] * 8,
        out_shape=[out] * 8,
        scratch_shapes=[pltpu.VMEM((1, NC), F32)],
        compiler_params=pltpu.CompilerParams(dimension_semantics=("arbitrary",),
                                             vmem_limit_bytes=VMEM_LIMIT),
        name="premix",
    )(*args)


def _rwkv_kernel(r_ref, k_ref, v_ref, lw_ref, kk_ref, a_ref, g_ref, rk_ref, lng_ref, lnb_ref,
                 o_ref, state_ref, y_ref, ra_ref, yu_ref, xbk_ref, xv_ref, gc_ref):
    @pl.when(pl.program_id(1) == 0)
    def _():
        state_ref[...] = jnp.zeros_like(state_ref)

    ts = r_ref.shape[0]
    n_pairs = r_ref.shape[1] // PAIR
    n_chunks = ts // CHUNK
    ti = lax.broadcasted_iota(jnp.int32, (CHUNK, CHUNK), 0)
    si = lax.broadcasted_iota(jnp.int32, (CHUNK, CHUNK), 1)
    tril = (ti >= si).astype(BF16)
    ri = lax.broadcasted_iota(jnp.int32, (2 * PAIR, 2 * PAIR), 0)
    ci = lax.broadcasted_iota(jnp.int32, (2 * PAIR, 2 * PAIR), 1)
    keep = (ri % CHUNK) + (ri >= PAIR).astype(jnp.int32) > (ci % CHUNK)
    ei = lax.broadcasted_iota(jnp.int32, (PAIR, PAIR), 0)
    ej = lax.broadcasted_iota(jnp.int32, (PAIR, PAIR), 1)
    eye = (ei == ej).astype(F32)
    lane = lax.broadcasted_iota(jnp.int32, (CHUNK, PAIR), 1)
    head0 = lane < HEAD_SIZE
    zeros_pp = jnp.zeros((PAIR, PAIR), BF16)

    def stack(t):
        return jnp.concatenate([jnp.where(head0, t, 0.0), jnp.where(head0, 0.0, t)], axis=0).astype(BF16)

    def pre_body(it, carry):
        chains = []
        for cc in range(RWKV_CHUNKS_PER_ITER):
            c = it * RWKV_CHUNKS_PER_ITER + cc
            rows = pl.ds(pl.multiple_of(c * CHUNK, CHUNK), CHUNK)
            lw = lw_ref[rows, :]
            cum = _dot_split(tril, lw, 3, split_lhs=False)
            g_in = jnp.exp(cum)
            g_ex = jnp.exp(cum - lw)
            g_inv = jnp.exp(-cum)
            gc_ref[c] = g_in[CHUNK - 1:CHUNK, :]
            kk = kk_ref[rows, :]
            rt = r_ref[rows, :] * g_in
            kt = k_ref[rows, :] * g_inv
            bt = (kk * a_ref[rows, :]) * g_inv
            at = -kk * g_ex
            vv = v_ref[rows, :]
            for p in range(n_pairs):
                ls = slice(p * PAIR, (p + 1) * PAIR)
                xa, xb, xk, xr, xv = (stack(t[:, ls]) for t in (at, bt, kt, rt, vv))
                xbk = jnp.concatenate([xb, xk], axis=0)
                xbk_ref[c, p] = xbk
                xv_ref[c, p] = xv
                chains.append(dict(c=c, p=p, xa=xa, xr=xr, xv=xv, xbk=xbk))
        for ch in chains:
            big = _dot(jnp.concatenate([ch["xa"], ch["xr"]], axis=0), ch["xbk"], _NT)
            big = jnp.where(keep, big, 0.0)
            ch["lab"] = big[0:PAIR, 0:PAIR]
            ch["lak"] = big[0:PAIR, PAIR:2 * PAIR].astype(BF16)
            ch["lrbk"] = big[PAIR:2 * PAIR, :].astype(BF16)
        for ch in chains:
            lb = ch["lab"].astype(BF16)
            ch["pw"] = _dot(lb, lb)
            ch["lakv"] = _dot(ch["lak"], ch["xv"]).astype(BF16)
            ch["inv"] = eye + ch["lab"]
        for i in range(1, 6):
            for ch in chains:
                pb = ch["pw"].astype(BF16)
                sb = ch["inv"].astype(BF16)
                if i < 5:
                    res = _dot(pb, jnp.concatenate([pb, sb], axis=1))
                    ch["pw"] = res[:, 0:PAIR]
                    ch["inv"] = ch["inv"] + res[:, PAIR:2 * PAIR]
                else:
                    ch["inv"] = ch["inv"] + _dot(pb, sb)
        for ch in chains:
            au = _dot(ch["inv"].astype(BF16), jnp.concatenate([ch["xa"], ch["lakv"]], axis=1))
            ch["a2"] = au[:, 0:PAIR].astype(BF16)
            ch["uv"] = au[:, PAIR:2 * PAIR]
        for ch in chains:
            rhs = jnp.concatenate(
                [jnp.concatenate([ch["a2"], ch["uv"].astype(BF16)], axis=1),
                 jnp.concatenate([zeros_pp, ch["xv"]], axis=1)], axis=0)
            ry = _dot(ch["lrbk"], rhs)
            r2 = ch["xr"].astype(F32) + ry[:, 0:PAIR]
            ra_ref[ch["c"], ch["p"]] = jnp.concatenate([r2.astype(BF16), ch["a2"]], axis=0)
            yu_ref[ch["c"], ch["p"]] = jnp.concatenate([ry[:, PAIR:2 * PAIR], ch["uv"]], axis=0)
        return carry

    lax.fori_loop(0, n_chunks // RWKV_CHUNKS_PER_ITER, pre_body, 0)

    def seq_body(c, carry):
        rows = pl.ds(pl.multiple_of(c * CHUNK, CHUNK), CHUNK)
        gc = gc_ref[c]
        res = [_dot(ra_ref[c, p], state_ref[p].astype(BF16), _NT) + yu_ref[c, p] for p in range(n_pairs)]
        for p in range(n_pairs):
            ls = slice(p * PAIR, (p + 1) * PAIR)
            y_st = res[p][0:PAIR]
            u_st = res[p][PAIR:2 * PAIR].astype(BF16)
            st = _dot(jnp.concatenate([u_st, xv_ref[c, p]], axis=0), xbk_ref[c, p], _TN)
            state_ref[p] = (state_ref[p] + st) * gc[:, ls]
            y_ref[rows, ls] = y_st[0:CHUNK] + y_st[CHUNK:2 * CHUNK]
        return carry

    lax.fori_loop(0, n_chunks, seq_body, 0)

    ones = _group_ones(RWKV_WIDTH, HEAD_SIZE).astype(BF16)
    y = y_ref[...]
    mean = _dot_split(y, ones, 2) * (1.0 / HEAD_SIZE)
    d = y - mean
    var = _dot_split(d * d, ones, 2) * (1.0 / HEAD_SIZE)
    yn = d * lax.rsqrt(var + GN_EPS) * lng_ref[...] + lnb_ref[...]
    bonus = _dot_split(r_ref[...] * k_ref[...] * rk_ref[...], ones, 2) * v_ref[...]
    o_ref[...] = (yn + bonus) * g_ref[...]


def _rwkv_call(r, k, v, lw, kk, a, g, r_k, lnx_g, lnx_b, batch):
    T, W = r.shape
    steps = T // batch // TS
    n_chunks, n_pairs = TS // CHUNK, W // PAIR
    tok = pl.BlockSpec((TS, W), lambda b, j: (b * steps + j, 0))
    row = pl.BlockSpec((1, W), lambda b, j: (0, 0))
    return pl.pallas_call(
        _rwkv_kernel,
        grid=(batch, steps),
        in_specs=[tok] * 7 + [row] * 3,
        out_specs=tok,
        out_shape=jax.ShapeDtypeStruct((T, W), F32),
        scratch_shapes=[pltpu.VMEM((n_pairs, PAIR, PAIR), F32), pltpu.VMEM((TS, W), F32),
                        pltpu.VMEM((n_chunks, n_pairs, 2 * PAIR, PAIR), BF16),
                        pltpu.VMEM((n_chunks, n_pairs, 2 * PAIR, PAIR), F32),
                        pltpu.VMEM((n_chunks, n_pairs, 2 * PAIR, PAIR), BF16),
                        pltpu.VMEM((n_chunks, n_pairs, PAIR, PAIR), BF16),
                        pltpu.VMEM((n_chunks, 1, W), F32)],
        compiler_params=pltpu.CompilerParams(dimension_semantics=("arbitrary", "arbitrary"),
                                             vmem_limit_bytes=VMEM_LIMIT),
        name="rwkv",
    )(r, k, v, lw, kk, a, g, r_k, lnx_g, lnx_b)


def _postmix_kernel(x_ref, y_ref, p_ref, mod_ref, pw_ref, ps_ref, wo_ref, ln_ref,
                    x_out, ht_out, carry_ref, *, tiles_per_batch):
    i = pl.program_id(0)
    j = i % tiles_per_batch

    @pl.when(j == 0)
    def _():
        carry_ref[...] = jnp.zeros_like(carry_ref)

    m = mod_ref[0]
    p = p_ref[...]
    tm = p.shape[0]
    ext = jnp.concatenate([carry_ref[...], p], axis=0)
    carry_ref[...] = p[tm - MAX_WINDOW:tm, :]
    pos = (j * tm + 1 + lax.broadcasted_iota(jnp.int32, (tm, 1), 0)).astype(F32)
    pooled = []
    for gi, win in enumerate(POOL_WINDOWS):
        ls = slice(gi * POOL_GROUP, (gi + 1) * POOL_GROUP)
        wsum = ext[:, ls]
        span = 1
        while span < win:
            wsum = wsum + pltpu.roll(wsum, span, 0)
            span *= 2
        mean = wsum[MAX_WINDOW:, :] / jnp.minimum(pos, float(win))
        d = mean - p[:, ls]
        pooled.append(_dot(d, pw_ref[gi], precision=HIGHEST))
    y_pool = jnp.concatenate(pooled, axis=1) * ps_ref[...]
    mix = (_dot(y_ref[...].astype(BF16), wo_ref[0:RWKV_WIDTH, :])
           + _dot(y_pool.astype(BF16), wo_ref[RWKV_WIDTH:RWKV_WIDTH + POOL_WIDTH, :]))
    x = x_ref[...] + m[2:3] * mix
    x_out[...] = x
    h2 = _rms_mod(x, ln_ref[...], m[3:4], m[4:5])
    ht_out[...] = h2.T.astype(BF16)


def _postmix_call(x2, y_rwkv, pool_in, modl, pool_w, pool_scale, w_out, ln_g, tiles_per_batch):
    T, D = x2.shape
    row = lambda n: pl.BlockSpec((1, n), lambda i: (0, 0))
    full = lambda a: pl.BlockSpec(a.shape, lambda i: (0,) * a.ndim)
    tok = lambda n: pl.BlockSpec((TM, n), lambda i: (i, 0))
    return pl.pallas_call(
        functools.partial(_postmix_kernel, tiles_per_batch=tiles_per_batch),
        grid=(T // TM,),
        in_specs=[tok(D), tok(RWKV_WIDTH), tok(POOL_WIDTH),
                  pl.BlockSpec((1, 6, D), lambda i: (i // tiles_per_batch, 0, 0)),
                  full(pool_w), row(POOL_WIDTH), full(w_out), row(D)],
        out_specs=[tok(D), pl.BlockSpec((D, TM), lambda i: (0, i))],
        out_shape=[jax.ShapeDtypeStruct((T, D), F32), jax.ShapeDtypeStruct((D, T), BF16)],
        scratch_shapes=[pltpu.VMEM((MAX_WINDOW, POOL_WIDTH), F32)],
        compiler_params=pltpu.CompilerParams(dimension_semantics=("arbitrary",),
                                             vmem_limit_bytes=VMEM_LIMIT),
        name="postmix",
    )(x2, y_rwkv, pool_in, modl, pool_w, pool_scale, w_out, ln_g)


def _extract16(cur, mark, marks, tie_break):
    n, w = cur.shape
    iota = lax.broadcasted_iota(jnp.int32, (n, w), 0).astype(F32)
    vals = []
    for j in range(PEER_TOPK):
        mx = jnp.max(cur, axis=0, keepdims=True)
        hit = cur == mx
        if tie_break:
            first = jnp.min(jnp.where(hit, iota, float(n)), axis=0, keepdims=True)
            hit = iota == first
        mark = jnp.where(hit, marks[j], mark)
        cur = jnp.where(hit, -jnp.inf, cur)
        vals.append(mx)
    removed = jnp.sum((cur == -jnp.inf).astype(F32), axis=0, keepdims=True)
    return mark, vals, removed


def _select_strip(s1, s2, outs, lanes, v1_ref, v2_ref, cand_ref, bm_ref, tie_break):
    rank2_out, e2_out, b_out, f_out = outs
    ranks = [float(j + 1) for j in range(PEER_TOPK)]
    unranked = jnp.full(s1.shape, 99.0, F32)
    rank1, vals1, removed1 = _extract16(s1, unranked, ranks, tie_break)
    rank2, vals2, removed2 = _extract16(s2, unranked, ranks, tie_break)
    for j in range(PEER_TOPK):
        v1_ref[j:j + 1, :] = vals1[j]
        v2_ref[j:j + 1, :] = vals2[j]
    w = s1.shape[1]
    for (a, nb), off in zip(_CAND_ROWS, _CAND_OFFS):
        cand_ref[off:off + nb, :] = vals1[a] + v2_ref[0:nb, :]
    cand_ref[N_CAND:N_CAND_PAD, :] = jnp.full((N_CAND_PAD - N_CAND, w), -jnp.inf, F32)
    cand = cand_ref[...]
    chosen, _, removed3 = _extract16(cand, jnp.zeros(cand.shape, F32), [1.0] * PEER_TOPK, tie_break)
    z = jnp.sum(chosen * jnp.exp(cand - cand[0:1, :]), axis=0, keepdims=True)
    cand_ref[...] = chosen
    for (a, nb), off in zip(_CAND_ROWS, _CAND_OFFS):
        bm_ref[a:a + 1, :] = jnp.sum(cand_ref[off:off + nb, :], axis=0, keepdims=True)
    bsel = jnp.zeros(s1.shape, F32)
    for a in range(PEER_TOPK):
        bsel = jnp.where(rank1 == ranks[a], bm_ref[a:a + 1, :], bsel)
    rank2_out[0, :, lanes] = rank2.astype(BF16)
    e2_out[0, :, lanes] = jnp.exp(s2 - vals2[0]).astype(BF16)
    b_out[0, :, lanes] = bsel
    f_out[0, :, lanes] = jnp.exp(s1 - vals1[0]) / z
    removed3 = removed3 - float(N_CAND_PAD - N_CAND)
    return jnp.max(jnp.maximum(jnp.maximum(removed1, removed2), removed3))


def _keyproj_kernel(keys_ref, q_ref, o_ref):
    o_ref[...] = _dot(keys_ref[0], q_ref[...], _NT, precision=HIGHEST).astype(BF16)


def _keyproj_call(keys, peer_q):
    D = peer_q.shape[0]
    n = keys.shape[0] * keys.shape[1]
    return pl.pallas_call(
        _keyproj_kernel,
        grid=(n,),
        in_specs=[pl.BlockSpec((1, N_KEYS, N_KEYS), lambda i: (i, 0, 0)),
                  pl.BlockSpec((D, N_KEYS), lambda i: (0, i))],
        out_specs=pl.BlockSpec((N_KEYS, D), lambda i: (i, 0)),
        out_shape=jax.ShapeDtypeStruct((n * N_KEYS, D), BF16),
        name="keyproj",
    )(keys.reshape(n, N_KEYS, N_KEYS), peer_q)


def _select_kernel(ht_ref, ws_ref, rank2_out, e2_out, b_out, f_out,
                   s1_ref, s2_ref, v1_ref, v2_ref, cand_ref, bm_ref):
    s = _dot(ws_ref[...], ht_ref[...])
    s1_ref[...] = s[0:N_KEYS]
    s2_ref[...] = s[N_KEYS:2 * N_KEYS]
    outs = (rank2_out, e2_out, b_out, f_out)

    def strip(j, carry):
        lanes = pl.ds(pl.multiple_of(j * SELECT_STRIP, SELECT_STRIP), SELECT_STRIP)
        s1 = s1_ref[:, lanes]
        s2 = s2_ref[:, lanes]
        most_removed = _select_strip(s1, s2, outs, lanes, v1_ref, v2_ref, cand_ref, bm_ref, False)

        @pl.when(most_removed > float(PEER_TOPK))
        def _():
            _select_strip(s1, s2, outs, lanes, v1_ref, v2_ref, cand_ref, bm_ref, True)

        return carry

    lax.fori_loop(0, s1_ref.shape[1] // SELECT_STRIP, strip, 0)


def _select_call(h2t, ws):
    D, T = h2t.shape
    out = jax.ShapeDtypeStruct((PEER_HEADS, N_KEYS, T), F32)
    out16 = jax.ShapeDtypeStruct((PEER_HEADS, N_KEYS, T), BF16)
    ospec = pl.BlockSpec((1, N_KEYS, TQ), lambda i, h: (h, 0, i))
    return pl.pallas_call(
        _select_kernel,
        grid=(T // TQ, PEER_HEADS),
        in_specs=[pl.BlockSpec((D, TQ), lambda i, h: (0, i)),
                  pl.BlockSpec((2 * N_KEYS, D), lambda i, h: (h, 0))],
        out_specs=[ospec] * 4,
        out_shape=[out16, out16, out, out],
        scratch_shapes=[pltpu.VMEM((N_KEYS, TQ), F32), pltpu.VMEM((N_KEYS, TQ), F32),
                        pltpu.VMEM((PEER_TOPK, SELECT_STRIP), F32), pltpu.VMEM((PEER_TOPK, SELECT_STRIP), F32),
                        pltpu.VMEM((N_CAND_PAD, SELECT_STRIP), F32),
                        pltpu.VMEM((PEER_TOPK, SELECT_STRIP), F32)],
        compiler_params=pltpu.CompilerParams(dimension_semantics=("arbitrary", "arbitrary"),
                                             vmem_limit_bytes=VMEM_LIMIT),
        name="select",
    )(h2t, ws)


def _experts_kernel(ht_ref, u_ref, vt_ref, rank2_ref, e2_ref, b_ref, f_ref, x_ref, mod_ref, lnf_ref,
                    x_out, acc_ref, *, tiles_per_batch, final):
    e = pl.program_id(1)

    @pl.when(e == 0)
    def _():
        acc_ref[...] = jnp.zeros_like(acc_ref)

    for sb in range(ht_ref.shape[1] // EXPERT_SUB):
        cols = slice(sb * EXPERT_SUB, (sb + 1) * EXPERT_SUB)
        gates = []
        for ii in range(FIRST_KEYS_PER_STEP):
            gate = None
            for h in range(PEER_HEADS):
                brow = jnp.broadcast_to(b_ref[h, e, ii:ii + 1, cols], (N_KEYS, EXPERT_SUB)).astype(BF16)
                frow = jnp.broadcast_to(f_ref[h, e, ii:ii + 1, cols], (N_KEYS, EXPERT_SUB)).astype(BF16)
                term = jnp.where(rank2_ref[h, :, cols] <= brow, e2_ref[h, :, cols], 0.0) * frow
                gate = term if gate is None else gate + term
            gates.append(gate)
        act = _dot(u_ref[...], ht_ref[:, cols])
        gelu = (0.5 * act * (1.0 + lax.erf(act * (2.0 ** -0.5)))).astype(BF16)
        z = jnp.concatenate(gates, axis=0) * gelu
        acc_ref[:, cols] += _dot(vt_ref[...], z)

    @pl.when(e == pl.num_programs(1) - 1)
    def _():
        m = mod_ref[0]
        x = x_ref[...] + m[5:6] * acc_ref[...].T
        if final:
            x = x * lax.rsqrt(jnp.mean(x * x, axis=-1, keepdims=True) + NORM_EPS) * lnf_ref[...]
        x_out[...] = x


def _experts_call(h2t, u_bf, vt_bf, rank2, e2, bsel, fsel, x2, modl, lnf_g, tiles_per_batch, final):
    D, T = h2t.shape
    NE = u_bf.shape[0]
    once = dict(pipeline_mode=pl.Buffered(1))
    sel = pl.BlockSpec((PEER_HEADS, N_KEYS, TT), lambda i, e: (0, 0, i), **once)
    bsel, fsel = (t.reshape(PEER_HEADS, N_KEYS // FIRST_KEYS_PER_STEP, FIRST_KEYS_PER_STEP, T)
                  for t in (bsel, fsel))
    sel1 = pl.BlockSpec(bsel.shape[:3] + (TT,), lambda i, e: (0, 0, 0, i), **once)
    tok = pl.BlockSpec((TT, D), lambda i, e: (i, 0))
    return pl.pallas_call(
        functools.partial(_experts_kernel, tiles_per_batch=tiles_per_batch, final=final),
        grid=(T // TT, NE // EB),
        in_specs=[pl.BlockSpec((D, TT), lambda i, e: (0, i), **once),
                  pl.BlockSpec((EB, D), lambda i, e: (e, 0)),
                  pl.BlockSpec((D, EB), lambda i, e: (0, e)),
                  sel, sel, sel1, sel1, pl.BlockSpec((TT, D), lambda i, e: (i, 0), **once),
                  pl.BlockSpec((1, 6, D), lambda i, e: (i // tiles_per_batch, 0, 0)),
                  pl.BlockSpec((1, D), lambda i, e: (0, 0))],
        out_specs=tok,
        out_shape=jax.ShapeDtypeStruct((T, D), F32),
        scratch_shapes=[pltpu.VMEM((D, TT), F32)],
        compiler_params=pltpu.CompilerParams(dimension_semantics=("arbitrary", "arbitrary"),
                                             vmem_limit_bytes=VMEM_LIMIT),
        name="experts",
    )(h2t, u_bf, vt_bf, rank2, e2, bsel, fsel, x2, modl, lnf_g)


def kernel(x, c, ada_w, ada_b, ln1_g, w_in, mu_shift, w0, w_up, a0, a_up, g_up, vres_down, vres_mu,
           vres_v0, vres_up, k_k, k_a, r_k, lnx_g, lnx_b, pool_w, pool_scale, w_out, ln2_g, peer_q,
           peer_keys, peer_u, peer_v, lnf_g):
    B, S, D = x.shape
    L = ada_w.shape[0]
    T = B * S
    assert S % TS == 0 and S % TM == 0 and T % TQ == 0 and T % TT == 0 and TT <= S
    row = lambda t: t.reshape(1, -1)

    c_pad = jnp.zeros((8, D), F32).at[:B].set(c)
    mod = _ada_call(c_pad, ada_w, ada_b)[:, :B].reshape(L, B, 6, D)

    half = LORA_COLS // 2
    x2 = x.reshape(T, D)
    v_first = None
    for l in range(L):
        if l == 0:
            w_full = w_in[l].astype(BF16)
            vres = None
        else:
            mv = vres_down.shape[-1]
            w_full = jnp.concatenate(
                [w_in[l], vres_down[l - 1], jnp.zeros((D, VRES_PAD - mv), F32)], axis=1).astype(BF16)
            vres = (jnp.zeros((1, VRES_PAD), F32).at[:, :mv].set(vres_mu[l - 1]),
                    row(vres_v0[l - 1]),
                    jnp.zeros((VRES_PAD, RWKV_WIDTH), F32).at[:mv].set(vres_up[l - 1]),
                    v_first)
        lora_w = jnp.zeros((LORA_COLS, 2 * RWKV_WIDTH), F32)
        lora_w = lora_w.at[:half, :RWKV_WIDTH].set(w_up[l]).at[half:, RWKV_WIDTH:].set(a_up[l])
        r, k2, v, lw, kk, a, g, pool_in = _premix_call(
            x2, mod[l], row(ln1_g[l]), w_full, row(mu_shift[l]), row(w0[l]), row(a0[l]), lora_w,
            g_up[l], row(k_k[l]), row(k_a[l]), vres, S // TM)
        if l == 0:
            v_first = v
        y_rwkv = _rwkv_call(r, k2, v, lw, kk, a, g, row(r_k[l]), row(lnx_g[l]), row(lnx_b[l]), B)
        x_mid, h2t = _postmix_call(x2, y_rwkv, pool_in, mod[l], pool_w[l], row(pool_scale[l]),
                                   w_out[l].astype(BF16), row(ln2_g[l]), S // TM)
        ws = _keyproj_call(peer_keys[l], peer_q[l])
        rank2, e2, bsel, fsel = _select_call(h2t, ws)
        x2 = _experts_call(h2t, peer_u[l].astype(BF16), peer_v[l].T.astype(BF16), rank2, e2, bsel, fsel,
                           x_mid, mod[l], row(lnf_g), S // TT, l == L - 1)
    return x2.reshape(B, S, D)
```

```python
import functools

import jax
import jax.numpy as jnp
from jax import lax
from jax.experimental import pallas as pl
from jax.experimental.pallas import tpu as pltpu

F32 = jnp.float32
BF16 = jnp.bfloat16
HIGHEST = lax.Precision.HIGHEST

HEAD_SIZE = 64
RWKV_WIDTH = 512
POOL_WIDTH = 512
POOL_GROUP = 128
POOL_WINDOWS = (2, 4, 8, 16)
MAX_WINDOW = 16
RWKV_COLS = 1792
LORA_COLS = 128
GATE_LORA = 128
VRES_PAD = 128
GN_EPS = 64e-5
NORM_EPS = 1e-6
N_KEYS = 128
PEER_HEADS = 8
PEER_TOPK = 16
CHUNK = 64
PAIR = 128
RWKV_CHUNKS_PER_ITER = 2

TM = 512
TS = 512
TQ = 512
SELECT_STRIP = 128
TT = 1024
EXPERT_SUB = 1024
EB = 1024
FIRST_KEYS_PER_STEP = EB // N_KEYS

VMEM_LIMIT = 56 * 1024 * 1024

_CAND_ROWS = [(a, PEER_TOPK // (a + 1)) for a in range(PEER_TOPK)]
_CAND_OFFS = []
_off = 0
for _a, _nb in _CAND_ROWS:
    _CAND_OFFS.append(_off)
    _off += _nb
N_CAND = _off
N_CAND_PAD = -(-N_CAND // 8) * 8


def _dot(a, b, dims=(((1,), (0,)), ((), ())), precision=None):
    return lax.dot_general(a, b, dims, precision=precision, preferred_element_type=F32)


_NT = (((1,), (1,)), ((), ()))
_TN = (((0,), (0,)), ((), ()))
_NN = (((1,), (0,)), ((), ()))


def _dot_split(a, b, pieces, split_lhs=True):
    rem = a if split_lhs else b
    acc = None
    for i in range(pieces):
        hi = rem.astype(BF16)
        part = _dot(hi, b, _NN) if split_lhs else _dot(a, hi, _NN)
        acc = part if acc is None else acc + part
        if i + 1 < pieces:
            rem = rem - hi.astype(F32)
    return acc


def _dot3(a, b):
    ah = a.astype(BF16)
    bh = b.astype(BF16)
    al = (a - ah.astype(F32)).astype(BF16)
    bl = (b - bh.astype(F32)).astype(BF16)
    return _dot(ah, bh) + (_dot(ah, bl) + _dot(al, bh))


def _group_ones(n, group):
    r = lax.broadcasted_iota(jnp.int32, (n, n), 0) // group
    c = lax.broadcasted_iota(jnp.int32, (n, n), 1) // group
    return (r == c).astype(F32)


def _rms_mod(x, g, shift, scale):
    y = x * lax.rsqrt(jnp.mean(x * x, axis=-1, keepdims=True) + NORM_EPS)
    return (y * g) * (1.0 + scale) + shift


def _ada_kernel(c_ref, w_ref, b_ref, o_ref):
    c = c_ref[...]
    ca = c * jax.nn.sigmoid(c)
    o_ref[0] = _dot(ca, w_ref[0], precision=HIGHEST) + b_ref[0]


def _ada_call(c_pad, ada_w, ada_b):
    L, D, D6 = ada_w.shape
    nb = D6 // D
    return pl.pallas_call(
        _ada_kernel,
        grid=(L, nb),
        in_specs=[
            pl.BlockSpec((8, D), lambda l, j: (0, 0)),
            pl.BlockSpec((1, D, D), lambda l, j: (l, 0, j)),
            pl.BlockSpec((1, 1, D), lambda l, j: (l, 0, j)),
        ],
        out_specs=pl.BlockSpec((1, 8, D), lambda l, j: (l, 0, j)),
        out_shape=jax.ShapeDtypeStruct((L, 8, D6), F32),
        name="adaln",
    )(c_pad, ada_w, ada_b.reshape(L, 1, D6))


def _premix_kernel(*refs, tiles_per_batch, has_vres):
    if has_vres:
        (x_ref, mod_ref, ln_ref, w_ref, mu_ref, w0_ref, a0_ref, lora_ref, gup_ref, kk_ref, ka_ref,
         vmu_ref, v0_ref, vup_ref, vfirst_ref,
         r_out, k_out, v_out, lw_out, kkn_out, a_out, g_out, p_out, carry_ref) = refs
    else:
        (x_ref, mod_ref, ln_ref, w_ref, mu_ref, w0_ref, a0_ref, lora_ref, gup_ref, kk_ref, ka_ref,
         r_out, k_out, v_out, lw_out, kkn_out, a_out, g_out, p_out, carry_ref) = refs
    i = pl.program_id(0)

    @pl.when(i % tiles_per_batch == 0)
    def _():
        carry_ref[...] = jnp.zeros_like(carry_ref)

    m = mod_ref[0]
    h = _rms_mod(x_ref[...], ln_ref[...], m[0:1], m[1:2])
    proj = _dot(h.astype(BF16), w_ref[...])
    tm = proj.shape[0]
    row0 = lax.broadcasted_iota(jnp.int32, (tm, 1), 0) == 0
    carry = carry_ref[...]
    carry_ref[...] = proj[tm - 1:tm, :]

    def shifted(lo, hi, mu):
        z = proj[:, lo:hi]
        prev = jnp.where(row0, carry[:, lo:hi], pltpu.roll(z, 1, 0))
        return z + (prev - z) * mu

    rw = shifted(0, RWKV_COLS, mu_ref[...])
    r = rw[:, 0:RWKV_WIDTH]
    k = rw[:, RWKV_WIDTH:2 * RWKV_WIDTH]
    v = rw[:, 2 * RWKV_WIDTH:3 * RWKV_WIDTH]
    lo = 3 * RWKV_WIDTH
    wa = rw[:, lo:lo + LORA_COLS]
    gd = rw[:, lo + LORA_COLS:lo + LORA_COLS + GATE_LORA]
    lane = lax.broadcasted_iota(jnp.int32, wa.shape, 1)
    wa = jnp.where(lane < LORA_COLS // 2, jnp.tanh(wa), wa)
    lora = _dot3(wa, lora_ref[...])
    zw = -(w0_ref[...] + lora[:, 0:RWKV_WIDTH])
    softplus = jnp.maximum(zw, 0.0) + jnp.log1p(jnp.exp(-jnp.abs(zw)))
    w = -softplus - 0.5
    a = jax.nn.sigmoid(a0_ref[...] + lora[:, RWKV_WIDTH:2 * RWKV_WIDTH])
    g = _dot3(jax.nn.sigmoid(gd), gup_ref[...])
    kk = k * kk_ref[...]
    ss = _dot_split(kk * kk, _group_ones(RWKV_WIDTH, HEAD_SIZE).astype(BF16), 3)
    kk = kk / jnp.maximum(jnp.sqrt(ss), 1e-12)
    k2 = k * (1.0 + (a - 1.0) * ka_ref[...])
    if has_vres:
        base = RWKV_COLS + POOL_WIDTH
        vd = shifted(base, base + VRES_PAD, vmu_ref[...])
        mixv = jax.nn.sigmoid(v0_ref[...] + _dot3(vd, vup_ref[...]))
        v = v + (vfirst_ref[...] - v) * mixv
    r_out[...] = r
    k_out[...] = k2
    v_out[...] = v
    lw_out[...] = -jnp.exp(w)
    kkn_out[...] = kk
    a_out[...] = a
    g_out[...] = g
    p_out[...] = proj[:, RWKV_COLS:RWKV_COLS + POOL_WIDTH]


def _premix_call(x2, modl, ln_g, w_full, mu, w0, a0, lora_w, g_up, k_k, k_a, vres, tiles_per_batch):
    T, D = x2.shape
    NC = w_full.shape[1]
    has_vres = vres is not None
    row = lambda n: pl.BlockSpec((1, n), lambda i: (0, 0))
    full = lambda a: pl.BlockSpec(a.shape, lambda i: (0,) * a.ndim)
    tok = lambda n: pl.BlockSpec((TM, n), lambda i: (i, 0))
    in_specs = [tok(D), pl.BlockSpec((1, 6, D), lambda i: (i // tiles_per_batch, 0, 0)), row(D),
                full(w_full), row(RWKV_COLS), row(RWKV_WIDTH), row(RWKV_WIDTH), full(lora_w), full(g_up),
                row(RWKV_WIDTH), row(RWKV_WIDTH)]
    args = [x2, modl, ln_g, w_full, mu, w0, a0, lora_w, g_up, k_k, k_a]
    if has_vres:
        vmu, v0, vup, vfirst = vres
        in_specs += [row(VRES_PAD), row(RWKV_WIDTH), full(vup), tok(RWKV_WIDTH)]
        args += [vmu, v0, vup, vfirst]
    out = jax.ShapeDtypeStruct((T, RWKV_WIDTH), F32)
    return pl.pallas_call(
        functools.partial(_premix_kernel, tiles_per_batch=tiles_per_batch, has_vres=has_vres),
        grid=(T // TM,),
        in_specs=in_specs,
        out_specs=[tok(RWKV_WIDTH)] * 8,
        out_shape=[out] * 8,
        scratch_shapes=[pltpu.VMEM((1, NC), F32)],
        compiler_params=pltpu.CompilerParams(dimension_semantics=("arbitrary",),
                                             vmem_limit_bytes=VMEM_LIMIT),
        name="premix",
    )(*args)


def _rwkv_kernel(r_ref, k_ref, v_ref, lw_ref, kk_ref, a_ref, g_ref, rk_ref, lng_ref, lnb_ref,
                 o_ref, state_ref, y_ref, ra_ref, yu_ref, xbk_ref, xv_ref, gc_ref):
    @pl.when(pl.program_id(1) == 0)
    def _():
        state_ref[...] = jnp.zeros_like(state_ref)

    ts = r_ref.shape[0]
    n_pairs = r_ref.shape[1] // PAIR
    n_chunks = ts // CHUNK
    ti = lax.broadcasted_iota(jnp.int32, (CHUNK, CHUNK), 0)
    si = lax.broadcasted_iota(jnp.int32, (CHUNK, CHUNK), 1)
    tril = (ti >= si).astype(BF16)
    ri = lax.broadcasted_iota(jnp.int32, (2 * PAIR, 2 * PAIR), 0)
    ci = lax.broadcasted_iota(jnp.int32, (2 * PAIR, 2 * PAIR), 1)
    keep = (ri % CHUNK) + (ri >= PAIR).astype(jnp.int32) > (ci % CHUNK)
    ei = lax.broadcasted_iota(jnp.int32, (PAIR, PAIR), 0)
    ej = lax.broadcasted_iota(jnp.int32, (PAIR, PAIR), 1)
    eye = (ei == ej).astype(F32)
    lane = lax.broadcasted_iota(jnp.int32, (CHUNK, PAIR), 1)
    head0 = lane < HEAD_SIZE
    zeros_pp = jnp.zeros((PAIR, PAIR), BF16)

    def stack(t):
        return jnp.concatenate([jnp.where(head0, t, 0.0), jnp.where(head0, 0.0, t)], axis=0).astype(BF16)

    def pre_body(it, carry):
        chains = []
        for cc in range(RWKV_CHUNKS_PER_ITER):
            c = it * RWKV_CHUNKS_PER_ITER + cc
            rows = pl.ds(pl.multiple_of(c * CHUNK, CHUNK), CHUNK)
            lw = lw_ref[rows, :]
            cum = _dot_split(tril, lw, 3, split_lhs=False)
            g_in = jnp.exp(cum)
            g_ex = jnp.exp(cum - lw)
            g_inv = jnp.exp(-cum)
            gc_ref[c] = g_in[CHUNK - 1:CHUNK, :]
            kk = kk_ref[rows, :]
            rt = r_ref[rows, :] * g_in
            kt = k_ref[rows, :] * g_inv
            bt = (kk * a_ref[rows, :]) * g_inv
            at = -kk * g_ex
            vv = v_ref[rows, :]
            for p in range(n_pairs):
                ls = slice(p * PAIR, (p + 1) * PAIR)
                xa, xb, xk, xr, xv = (stack(t[:, ls]) for t in (at, bt, kt, rt, vv))
                xbk = jnp.concatenate([xb, xk], axis=0)
                xbk_ref[c, p] = xbk
                xv_ref[c, p] = xv
                chains.append(dict(c=c, p=p, xa=xa, xr=xr, xv=xv, xbk=xbk))
        for ch in chains:
            big = _dot(jnp.concatenate([ch["xa"], ch["xr"]], axis=0), ch["xbk"], _NT)
            big = jnp.where(keep, big, 0.0)
            ch["lab"] = big[0:PAIR, 0:PAIR]
            ch["lak"] = big[0:PAIR, PAIR:2 * PAIR].astype(BF16)
            ch["lrbk"] = big[PAIR:2 * PAIR, :].astype(BF16)
        for ch in chains:
            lb = ch["lab"].astype(BF16)
            ch["pw"] = _dot(lb, lb)
            ch["lakv"] = _dot(ch["lak"], ch["xv"]).astype(BF16)
            ch["inv"] = eye + ch["lab"]
        for i in range(1, 6):
            for ch in chains:
                pb = ch["pw"].astype(BF16)
                sb = ch["inv"].astype(BF16)
                if i < 5:
                    res = _dot(pb, jnp.concatenate([pb, sb], axis=1))
                    ch["pw"] = res[:, 0:PAIR]
                    ch["inv"] = ch["inv"] + res[:, PAIR:2 * PAIR]
                else:
                    ch["inv"] = ch["inv"] + _dot(pb, sb)
        for ch in chains:
            au = _dot(ch["inv"].astype(BF16), jnp.concatenate([ch["xa"], ch["lakv"]], axis=1))
            ch["a2"] = au[:, 0:PAIR].astype(BF16)
            ch["uv"] = au[:, PAIR:2 * PAIR]
        for ch in chains:
            rhs = jnp.concatenate(
                [jnp.concatenate([ch["a2"], ch["uv"].astype(BF16)], axis=1),
                 jnp.concatenate([zeros_pp, ch["xv"]], axis=1)], axis=0)
            ry = _dot(ch["lrbk"], rhs)
            r2 = ch["xr"].astype(F32) + ry[:, 0:PAIR]
            ra_ref[ch["c"], ch["p"]] = jnp.concatenate([r2.astype(BF16), ch["a2"]], axis=0)
            yu_ref[ch["c"], ch["p"]] = jnp.concatenate([ry[:, PAIR:2 * PAIR], ch["uv"]], axis=0)
        return carry

    lax.fori_loop(0, n_chunks // RWKV_CHUNKS_PER_ITER, pre_body, 0)

    def seq_body(c, carry):
        rows = pl.ds(pl.multiple_of(c * CHUNK, CHUNK), CHUNK)
        gc = gc_ref[c]
        res = [_dot(ra_ref[c, p], state_ref[p].astype(BF16), _NT) + yu_ref[c, p] for p in range(n_pairs)]
        for p in range(n_pairs):
            ls = slice(p * PAIR, (p + 1) * PAIR)
            y_st = res[p][0:PAIR]
            u_st = res[p][PAIR:2 * PAIR].astype(BF16)
            st = _dot(jnp.concatenate([u_st, xv_ref[c, p]], axis=0), xbk_ref[c, p], _TN)
            state_ref[p] = (state_ref[p] + st) * gc[:, ls]
            y_ref[rows, ls] = y_st[0:CHUNK] + y_st[CHUNK:2 * CHUNK]
        return carry

    lax.fori_loop(0, n_chunks, seq_body, 0)

    ones = _group_ones(RWKV_WIDTH, HEAD_SIZE).astype(BF16)
    y = y_ref[...]
    mean = _dot_split(y, ones, 2) * (1.0 / HEAD_SIZE)
    d = y - mean
    var = _dot_split(d * d, ones, 2) * (1.0 / HEAD_SIZE)
    yn = d * lax.rsqrt(var + GN_EPS) * lng_ref[...] + lnb_ref[...]
    bonus = _dot_split(r_ref[...] * k_ref[...] * rk_ref[...], ones, 2) * v_ref[...]
    o_ref[...] = (yn + bonus) * g_ref[...]


def _rwkv_call(r, k, v, lw, kk, a, g, r_k, lnx_g, lnx_b, batch):
    T, W = r.shape
    steps = T // batch // TS
    n_chunks, n_pairs = TS // CHUNK, W // PAIR
    tok = pl.BlockSpec((TS, W), lambda b, j: (b * steps + j, 0))
    row = pl.BlockSpec((1, W), lambda b, j: (0, 0))
    return pl.pallas_call(
        _rwkv_kernel,
        grid=(batch, steps),
        in_specs=[tok] * 7 + [row] * 3,
        out_specs=tok,
        out_shape=jax.ShapeDtypeStruct((T, W), F32),
        scratch_shapes=[pltpu.VMEM((n_pairs, PAIR, PAIR), F32), pltpu.VMEM((TS, W), F32),
                        pltpu.VMEM((n_chunks, n_pairs, 2 * PAIR, PAIR), BF16),
                        pltpu.VMEM((n_chunks, n_pairs, 2 * PAIR, PAIR), F32),
                        pltpu.VMEM((n_chunks, n_pairs, 2 * PAIR, PAIR), BF16),
                        pltpu.VMEM((n_chunks, n_pairs, PAIR, PAIR), BF16),
                        pltpu.VMEM((n_chunks, 1, W), F32)],
        compiler_params=pltpu.CompilerParams(dimension_semantics=("arbitrary", "arbitrary"),
                                             vmem_limit_bytes=VMEM_LIMIT),
        name="rwkv",
    )(r, k, v, lw, kk, a, g, r_k, lnx_g, lnx_b)


def _postmix_kernel(x_ref, y_ref, p_ref, mod_ref, pw_ref, ps_ref, wo_ref, ln_ref,
                    x_out, ht_out, carry_ref, *, tiles_per_batch):
    i = pl.program_id(0)
    j = i % tiles_per_batch

    @pl.when(j == 0)
    def _():
        carry_ref[...] = jnp.zeros_like(carry_ref)

    m = mod_ref[0]
    p = p_ref[...]
    tm = p.shape[0]
    ext = jnp.concatenate([carry_ref[...], p], axis=0)
    carry_ref[...] = p[tm - MAX_WINDOW:tm, :]
    pos = (j * tm + 1 + lax.broadcasted_iota(jnp.int32, (tm, 1), 0)).astype(F32)
    pooled = []
    for gi, win in enumerate(POOL_WINDOWS):
        ls = slice(gi * POOL_GROUP, (gi + 1) * POOL_GROUP)
        wsum = ext[:, ls]
        span = 1
        while span < win:
            wsum = wsum + pltpu.roll(wsum, span, 0)
            span *= 2
        mean = wsum[MAX_WINDOW:, :] / jnp.minimum(pos, float(win))
        d = mean - p[:, ls]
        pooled.append(_dot3(d, pw_ref[gi]))
    y_pool = jnp.concatenate(pooled, axis=1) * ps_ref[...]
    mix = (_dot(y_ref[...].astype(BF16), wo_ref[0:RWKV_WIDTH, :])
           + _dot(y_pool.astype(BF16), wo_ref[RWKV_WIDTH:RWKV_WIDTH + POOL_WIDTH, :]))
    x = x_ref[...] + m[2:3] * mix
    x_out[...] = x
    h2 = _rms_mod(x, ln_ref[...], m[3:4], m[4:5])
    ht_out[...] = h2.T.astype(BF16)


def _postmix_call(x2, y_rwkv, pool_in, modl, pool_w, pool_scale, w_out, ln_g, tiles_per_batch):
    T, D = x2.shape
    row = lambda n: pl.BlockSpec((1, n), lambda i: (0, 0))
    full = lambda a: pl.BlockSpec(a.shape, lambda i: (0,) * a.ndim)
    tok = lambda n: pl.BlockSpec((TM, n), lambda i: (i, 0))
    return pl.pallas_call(
        functools.partial(_postmix_kernel, tiles_per_batch=tiles_per_batch),
        grid=(T // TM,),
        in_specs=[tok(D), tok(RWKV_WIDTH), tok(POOL_WIDTH),
                  pl.BlockSpec((1, 6, D), lambda i: (i // tiles_per_batch, 0, 0)),
                  full(pool_w), row(POOL_WIDTH), full(w_out), row(D)],
        out_specs=[tok(D), pl.BlockSpec((D, TM), lambda i: (0, i))],
        out_shape=[jax.ShapeDtypeStruct((T, D), F32), jax.ShapeDtypeStruct((D, T), BF16)],
        scratch_shapes=[pltpu.VMEM((MAX_WINDOW, POOL_WIDTH), F32)],
        compiler_params=pltpu.CompilerParams(dimension_semantics=("arbitrary",),
                                             vmem_limit_bytes=VMEM_LIMIT),
        name="postmix",
    )(x2, y_rwkv, pool_in, modl, pool_w, pool_scale, w_out, ln_g)


def _extract16(cur, mark, marks, tie_break):
    n, w = cur.shape
    iota = lax.broadcasted_iota(jnp.int32, (n, w), 0).astype(F32)
    vals = []
    for j in range(PEER_TOPK):
        mx = jnp.max(cur, axis=0, keepdims=True)
        hit = cur == mx
        if tie_break:
            first = jnp.min(jnp.where(hit, iota, float(n)), axis=0, keepdims=True)
            hit = iota == first
        mark = jnp.where(hit, marks[j], mark)
        cur = jnp.where(hit, -jnp.inf, cur)
        vals.append(mx)
    removed = jnp.sum((cur == -jnp.inf).astype(F32), axis=0, keepdims=True)
    return mark, vals, removed


def _select_strip(s1, s2, outs, lanes, v1_ref, v2_ref, cand_ref, bm_ref, tie_break):
    rank2_out, e2_out, b_out, f_out = outs
    ranks = [float(j + 1) for j in range(PEER_TOPK)]
    unranked = jnp.full(s1.shape, 99.0, F32)
    rank1, vals1, removed1 = _extract16(s1, unranked, ranks, tie_break)
    rank2, vals2, removed2 = _extract16(s2, unranked, ranks, tie_break)
    for j in range(PEER_TOPK):
        v1_ref[j:j + 1, :] = vals1[j]
        v2_ref[j:j + 1, :] = vals2[j]
    w = s1.shape[1]
    for (a, nb), off in zip(_CAND_ROWS, _CAND_OFFS):
        cand_ref[off:off + nb, :] = vals1[a] + v2_ref[0:nb, :]
    cand_ref[N_CAND:N_CAND_PAD, :] = jnp.full((N_CAND_PAD - N_CAND, w), -jnp.inf, F32)
    cand = cand_ref[...]
    chosen, _, removed3 = _extract16(cand, jnp.zeros(cand.shape, F32), [1.0] * PEER_TOPK, tie_break)
    z = jnp.sum(chosen * jnp.exp(cand - cand[0:1, :]), axis=0, keepdims=True)
    cand_ref[...] = chosen
    for (a, nb), off in zip(_CAND_ROWS, _CAND_OFFS):
        bm_ref[a:a + 1, :] = jnp.sum(cand_ref[off:off + nb, :], axis=0, keepdims=True)
    bsel = jnp.zeros(s1.shape, F32)
    for a in range(PEER_TOPK):
        bsel = jnp.where(rank1 == ranks[a], bm_ref[a:a + 1, :], bsel)
    rank2_out[0, :, lanes] = rank2.astype(BF16)
    e2_out[0, :, lanes] = jnp.exp(s2 - vals2[0]).astype(BF16)
    b_out[0, :, lanes] = bsel
    f_out[0, :, lanes] = jnp.exp(s1 - vals1[0]) / z
    removed3 = removed3 - float(N_CAND_PAD - N_CAND)
    return jnp.max(jnp.maximum(jnp.maximum(removed1, removed2), removed3))


def _keyproj_kernel(keys_ref, q_ref, o_ref):
    o_ref[...] = _dot(keys_ref[0], q_ref[...], _NT, precision=HIGHEST).astype(BF16)


def _keyproj_call(keys, peer_q):
    D = peer_q.shape[0]
    n = keys.shape[0] * keys.shape[1]
    return pl.pallas_call(
        _keyproj_kernel,
        grid=(n,),
        in_specs=[pl.BlockSpec((1, N_KEYS, N_KEYS), lambda i: (i, 0, 0)),
                  pl.BlockSpec((D, N_KEYS), lambda i: (0, i))],
        out_specs=pl.BlockSpec((N_KEYS, D), lambda i: (i, 0)),
        out_shape=jax.ShapeDtypeStruct((n * N_KEYS, D), BF16),
        name="keyproj",
    )(keys.reshape(n, N_KEYS, N_KEYS), peer_q)


def _select_kernel(ht_ref, ws_ref, rank2_out, e2_out, b_out, f_out,
                   s1_ref, s2_ref, v1_ref, v2_ref, cand_ref, bm_ref):
    s = _dot(ws_ref[...], ht_ref[...])
    s1_ref[...] = s[0:N_KEYS]
    s2_ref[...] = s[N_KEYS:2 * N_KEYS]
    outs = (rank2_out, e2_out, b_out, f_out)

    def strip(j, carry):
        lanes = pl.ds(pl.multiple_of(j * SELECT_STRIP, SELECT_STRIP), SELECT_STRIP)
        s1 = s1_ref[:, lanes]
        s2 = s2_ref[:, lanes]
        most_removed = _select_strip(s1, s2, outs, lanes, v1_ref, v2_ref, cand_ref, bm_ref, False)

        @pl.when(most_removed > float(PEER_TOPK))
        def _():
            _select_strip(s1, s2, outs, lanes, v1_ref, v2_ref, cand_ref, bm_ref, True)

        return carry

    lax.fori_loop(0, s1_ref.shape[1] // SELECT_STRIP, strip, 0)


def _select_call(h2t, ws):
    D, T = h2t.shape
    out = jax.ShapeDtypeStruct((PEER_HEADS, N_KEYS, T), F32)
    out16 = jax.ShapeDtypeStruct((PEER_HEADS, N_KEYS, T), BF16)
    ospec = pl.BlockSpec((1, N_KEYS, TQ), lambda i, h: (h, 0, i))
    return pl.pallas_call(
        _select_kernel,
        grid=(T // TQ, PEER_HEADS),
        in_specs=[pl.BlockSpec((D, TQ), lambda i, h: (0, i)),
                  pl.BlockSpec((2 * N_KEYS, D), lambda i, h: (h, 0))],
        out_specs=[ospec] * 4,
        out_shape=[out16, out16, out, out],
        scratch_shapes=[pltpu.VMEM((N_KEYS, TQ), F32), pltpu.VMEM((N_KEYS, TQ), F32),
                        pltpu.VMEM((PEER_TOPK, SELECT_STRIP), F32), pltpu.VMEM((PEER_TOPK, SELECT_STRIP), F32),
                        pltpu.VMEM((N_CAND_PAD, SELECT_STRIP), F32),
                        pltpu.VMEM((PEER_TOPK, SELECT_STRIP), F32)],
        compiler_params=pltpu.CompilerParams(dimension_semantics=("arbitrary", "arbitrary"),
                                             vmem_limit_bytes=VMEM_LIMIT),
        name="select",
    )(h2t, ws)


def _experts_kernel(ht_ref, u_ref, vt_ref, rank2_ref, e2_ref, b_ref, f_ref, x_ref, mod_ref, lnf_ref,
                    x_out, acc_ref, *, tiles_per_batch, final):
    e = pl.program_id(1)

    @pl.when(e == 0)
    def _():
        acc_ref[...] = jnp.zeros_like(acc_ref)

    for sb in range(ht_ref.shape[1] // EXPERT_SUB):
        cols = slice(sb * EXPERT_SUB, (sb + 1) * EXPERT_SUB)
        gates = []
        for ii in range(FIRST_KEYS_PER_STEP):
            gate = None
            for h in range(PEER_HEADS):
                brow = jnp.broadcast_to(b_ref[h, e, ii:ii + 1, cols], (N_KEYS, EXPERT_SUB)).astype(BF16)
                frow = jnp.broadcast_to(f_ref[h, e, ii:ii + 1, cols], (N_KEYS, EXPERT_SUB)).astype(BF16)
                term = jnp.where(rank2_ref[h, :, cols] <= brow, e2_ref[h, :, cols], 0.0) * frow
                gate = term if gate is None else gate + term
            gates.append(gate)
        act = _dot(u_ref[...], ht_ref[:, cols]).astype(BF16)
        gelu = 0.5 * act * (1.0 + lax.erf(act * (2.0 ** -0.5)))
        z = jnp.concatenate(gates, axis=0) * gelu
        acc_ref[:, cols] += _dot(vt_ref[...], z)

    @pl.when(e == pl.num_programs(1) - 1)
    def _():
        m = mod_ref[0]
        x = x_ref[...] + m[5:6] * acc_ref[...].T
        if final:
            x = x * lax.rsqrt(jnp.mean(x * x, axis=-1, keepdims=True) + NORM_EPS) * lnf_ref[...]
        x_out[...] = x


def _experts_call(h2t, u_bf, vt_bf, rank2, e2, bsel, fsel, x2, modl, lnf_g, tiles_per_batch, final):
    D, T = h2t.shape
    NE = u_bf.shape[0]
    once = dict(pipeline_mode=pl.Buffered(1))
    sel = pl.BlockSpec((PEER_HEADS, N_KEYS, TT), lambda i, e: (0, 0, i), **once)
    bsel, fsel = (t.reshape(PEER_HEADS, N_KEYS // FIRST_KEYS_PER_STEP, FIRST_KEYS_PER_STEP, T)
                  for t in (bsel, fsel))
    sel1 = pl.BlockSpec(bsel.shape[:3] + (TT,), lambda i, e: (0, 0, 0, i), **once)
    tok = pl.BlockSpec((TT, D), lambda i, e: (i, 0))
    return pl.pallas_call(
        functools.partial(_experts_kernel, tiles_per_batch=tiles_per_batch, final=final),
        grid=(T // TT, NE // EB),
        in_specs=[pl.BlockSpec((D, TT), lambda i, e: (0, i), **once),
                  pl.BlockSpec((EB, D), lambda i, e: (e, 0)),
                  pl.BlockSpec((D, EB), lambda i, e: (0, e)),
                  sel, sel, sel1, sel1, pl.BlockSpec((TT, D), lambda i, e: (i, 0), **once),
                  pl.BlockSpec((1, 6, D), lambda i, e: (i // tiles_per_batch, 0, 0)),
                  pl.BlockSpec((1, D), lambda i, e: (0, 0))],
        out_specs=tok,
        out_shape=jax.ShapeDtypeStruct((T, D), F32),
        scratch_shapes=[pltpu.VMEM((D, TT), F32)],
        compiler_params=pltpu.CompilerParams(dimension_semantics=("arbitrary", "arbitrary"),
                                             vmem_limit_bytes=VMEM_LIMIT),
        name="experts",
    )(h2t, u_bf, vt_bf, rank2, e2, bsel, fsel, x2, modl, lnf_g)


def kernel(x, c, ada_w, ada_b, ln1_g, w_in, mu_shift, w0, w_up, a0, a_up, g_up, vres_down, vres_mu,
           vres_v0, vres_up, k_k, k_a, r_k, lnx_g, lnx_b, pool_w, pool_scale, w_out, ln2_g, peer_q,
           peer_keys, peer_u, peer_v, lnf_g):
    B, S, D = x.shape
    L = ada_w.shape[0]
    T = B * S
    assert S % TS == 0 and S % TM == 0 and T % TQ == 0 and T % TT == 0 and TT <= S
    row = lambda t: t.reshape(1, -1)

    c_pad = jnp.zeros((8, D), F32).at[:B].set(c)
    mod = _ada_call(c_pad, ada_w, ada_b)[:, :B].reshape(L, B, 6, D)

    half = LORA_COLS // 2
    x2 = x.reshape(T, D)
    v_first = None
    for l in range(L):
        if l == 0:
            w_full = w_in[l].astype(BF16)
            vres = None
        else:
            mv = vres_down.shape[-1]
            w_full = jnp.concatenate(
                [w_in[l], vres_down[l - 1], jnp.zeros((D, VRES_PAD - mv), F32)], axis=1).astype(BF16)
            vres = (jnp.zeros((1, VRES_PAD), F32).at[:, :mv].set(vres_mu[l - 1]),
                    row(vres_v0[l - 1]),
                    jnp.zeros((VRES_PAD, RWKV_WIDTH), F32).at[:mv].set(vres_up[l - 1]),
                    v_first)
        lora_w = jnp.zeros((LORA_COLS, 2 * RWKV_WIDTH), F32)
        lora_w = lora_w.at[:half, :RWKV_WIDTH].set(w_up[l]).at[half:, RWKV_WIDTH:].set(a_up[l])
        r, k2, v, lw, kk, a, g, pool_in = _premix_call(
            x2, mod[l], row(ln1_g[l]), w_full, row(mu_shift[l]), row(w0[l]), row(a0[l]), lora_w,
            g_up[l], row(k_k[l]), row(k_a[l]), vres, S // TM)
        if l == 0:
            v_first = v
        y_rwkv = _rwkv_call(r, k2, v, lw, kk, a, g, row(r_k[l]), row(lnx_g[l]), row(lnx_b[l]), B)
        x_mid, h2t = _postmix_call(x2, y_rwkv, pool_in, mod[l], pool_w[l], row(pool_scale[l]),
                                   w_out[l].astype(BF16), row(ln2_g[l]), S // TM)
        ws = _keyproj_call(peer_keys[l], peer_q[l])
        rank2, e2, bsel, fsel = _select_call(h2t, ws)
        x2 = _experts_call(h2t, peer_u[l].astype(BF16), peer_v[l].T.astype(BF16), rank2, e2, bsel, fsel,
                           x_mid, mod[l], row(lnf_g), S // TT, l == L - 1)
    return x2.reshape(B, S, D)
```

```python
import functools

import jax
import jax.numpy as jnp
from jax import lax
from jax.experimental import pallas as pl
from jax.experimental.pallas import tpu as pltpu

F32 = jnp.float32
BF16 = jnp.bfloat16
HIGHEST = lax.Precision.HIGHEST

HEAD_SIZE = 64
RWKV_WIDTH = 512
POOL_WIDTH = 512
POOL_GROUP = 128
POOL_WINDOWS = (2, 4, 8, 16)
MAX_WINDOW = 16
RWKV_COLS = 1792
LORA_COLS = 128
GATE_LORA = 128
VRES_PAD = 128
GN_EPS = 64e-5
NORM_EPS = 1e-6
N_KEYS = 128
PEER_HEADS = 8
PEER_TOPK = 16
CHUNK = 64
PAIR = 128
RWKV_CHUNKS_PER_ITER = 2

TM = 512
TS = 512
TQ = 512
SELECT_STRIP = 128
TT = 1024
EXPERT_SUB = 1024
EB = 1024
FIRST_KEYS_PER_STEP = EB // N_KEYS

VMEM_LIMIT = 56 * 1024 * 1024

_CAND_ROWS = [(a, PEER_TOPK // (a + 1)) for a in range(PEER_TOPK)]
_CAND_OFFS = []
_off = 0
for _a, _nb in _CAND_ROWS:
    _CAND_OFFS.append(_off)
    _off += _nb
N_CAND = _off
N_CAND_PAD = -(-N_CAND // 8) * 8


def _dot(a, b, dims=(((1,), (0,)), ((), ())), precision=None):
    return lax.dot_general(a, b, dims, precision=precision, preferred_element_type=F32)


_NT = (((1,), (1,)), ((), ()))
_TN = (((0,), (0,)), ((), ()))
_NN = (((1,), (0,)), ((), ()))


def _dot_split(a, b, pieces, split_lhs=True):
    rem = a if split_lhs else b
    acc = None
    for i in range(pieces):
        hi = rem.astype(BF16)
        part = _dot(hi, b, _NN) if split_lhs else _dot(a, hi, _NN)
        acc = part if acc is None else acc + part
        if i + 1 < pieces:
            rem = rem - hi.astype(F32)
    return acc


def _dot3(a, b):
    ah = a.astype(BF16)
    bh = b.astype(BF16)
    al = (a - ah.astype(F32)).astype(BF16)
    bl = (b - bh.astype(F32)).astype(BF16)
    return _dot(ah, bh) + (_dot(ah, bl) + _dot(al, bh))


def _group_ones(n, group):
    r = lax.broadcasted_iota(jnp.int32, (n, n), 0) // group
    c = lax.broadcasted_iota(jnp.int32, (n, n), 1) // group
    return (r == c).astype(F32)


def _rms_mod(x, g, shift, scale):
    y = x * lax.rsqrt(jnp.mean(x * x, axis=-1, keepdims=True) + NORM_EPS)
    return (y * g) * (1.0 + scale) + shift


def _ada_kernel(c_ref, w_ref, b_ref, o_ref):
    c = c_ref[...]
    ca = c * jax.nn.sigmoid(c)
    o_ref[0] = _dot(ca, w_ref[0], precision=HIGHEST) + b_ref[0]


def _ada_call(c_pad, ada_w, ada_b):
    L, D, D6 = ada_w.shape
    nb = D6 // D
    return pl.pallas_call(
        _ada_kernel,
        grid=(L, nb),
        in_specs=[
            pl.BlockSpec((8, D), lambda l, j: (0, 0)),
            pl.BlockSpec((1, D, D), lambda l, j: (l, 0, j)),
            pl.BlockSpec((1, 1, D), lambda l, j: (l, 0, j)),
        ],
        out_specs=pl.BlockSpec((1, 8, D), lambda l, j: (l, 0, j)),
        out_shape=jax.ShapeDtypeStruct((L, 8, D6), F32),
        name="adaln",
    )(c_pad, ada_w, ada_b.reshape(L, 1, D6))


def _premix_kernel(*refs, tiles_per_batch, has_vres):
    if has_vres:
        (x_ref, mod_ref, ln_ref, w_ref, mu_ref, w0_ref, a0_ref, lora_ref, gup_ref, kk_ref, ka_ref,
         vmu_ref, v0_ref, vup_ref, vfirst_ref,
         r_out, k_out, v_out, lw_out, kkn_out, a_out, g_out, p_out, carry_ref) = refs
    else:
        (x_ref, mod_ref, ln_ref, w_ref, mu_ref, w0_ref, a0_ref, lora_ref, gup_ref, kk_ref, ka_ref,
         r_out, k_out, v_out, lw_out, kkn_out, a_out, g_out, p_out, carry_ref) = refs
    i = pl.program_id(0)

    @pl.when(i % tiles_per_batch == 0)
    def _():
        carry_ref[...] = jnp.zeros_like(carry_ref)

    m = mod_ref[0]
    h = _rms_mod(x_ref[...], ln_ref[...], m[0:1], m[1:2])
    proj = _dot(h.astype(BF16), w_ref[...])
    tm = proj.shape[0]
    row0 = lax.broadcasted_iota(jnp.int32, (tm, 1), 0) == 0
    carry = carry_ref[...]
    carry_ref[...] = proj[tm - 1:tm, :]

    def shifted(lo, hi, mu):
        z = proj[:, lo:hi]
        prev = jnp.where(row0, carry[:, lo:hi], pltpu.roll(z, 1, 0))
        return z + (prev - z) * mu

    rw = shifted(0, RWKV_COLS, mu_ref[...])
    r = rw[:, 0:RWKV_WIDTH]
    k = rw[:, RWKV_WIDTH:2 * RWKV_WIDTH]
    v = rw[:, 2 * RWKV_WIDTH:3 * RWKV_WIDTH]
    lo = 3 * RWKV_WIDTH
    wa = rw[:, lo:lo + LORA_COLS]
    gd = rw[:, lo + LORA_COLS:lo + LORA_COLS + GATE_LORA]
    lane = lax.broadcasted_iota(jnp.int32, wa.shape, 1)
    wa = jnp.where(lane < LORA_COLS // 2, jnp.tanh(wa), wa)
    lora = _dot3(wa, lora_ref[...])
    zw = -(w0_ref[...] + lora[:, 0:RWKV_WIDTH])
    softplus = jnp.maximum(zw, 0.0) + jnp.log1p(jnp.exp(-jnp.abs(zw)))
    w = -softplus - 0.5
    a = jax.nn.sigmoid(a0_ref[...] + lora[:, RWKV_WIDTH:2 * RWKV_WIDTH])
    g = _dot3(jax.nn.sigmoid(gd), gup_ref[...])
    kk = k * kk_ref[...]
    ss = _dot_split(kk * kk, _group_ones(RWKV_WIDTH, HEAD_SIZE).astype(BF16), 3)
    kk = kk / jnp.maximum(jnp.sqrt(ss), 1e-12)
    k2 = k * (1.0 + (a - 1.0) * ka_ref[...])
    if has_vres:
        base = RWKV_COLS + POOL_WIDTH
        vd = shifted(base, base + VRES_PAD, vmu_ref[...])
        mixv = jax.nn.sigmoid(v0_ref[...] + _dot3(vd, vup_ref[...]))
        v = v + (vfirst_ref[...] - v) * mixv
    r_out[...] = r
    k_out[...] = k2
    v_out[...] = v
    lw_out[...] = -jnp.exp(w)
    kkn_out[...] = kk
    a_out[...] = a
    g_out[...] = g
    p_out[...] = proj[:, RWKV_COLS:RWKV_COLS + POOL_WIDTH]


def _premix_call(x2, modl, ln_g, w_full, mu, w0, a0, lora_w, g_up, k_k, k_a, vres, tiles_per_batch):
    T, D = x2.shape
    NC = w_full.shape[1]
    has_vres = vres is not None
    row = lambda n: pl.BlockSpec((1, n), lambda i: (0, 0))
    full = lambda a: pl.BlockSpec(a.shape, lambda i: (0,) * a.ndim)
    tok = lambda n: pl.BlockSpec((TM, n), lambda i: (i, 0))
    in_specs = [tok(D), pl.BlockSpec((1, 6, D), lambda i: (i // tiles_per_batch, 0, 0)), row(D),
                full(w_full), row(RWKV_COLS), row(RWKV_WIDTH), row(RWKV_WIDTH), full(lora_w), full(g_up),
                row(RWKV_WIDTH), row(RWKV_WIDTH)]
    args = [x2, modl, ln_g, w_full, mu, w0, a0, lora_w, g_up, k_k, k_a]
    if has_vres:
        vmu, v0, vup, vfirst = vres
        in_specs += [row(VRES_PAD), row(RWKV_WIDTH), full(vup), tok(RWKV_WIDTH)]
        args += [vmu, v0, vup, vfirst]
    out = jax.ShapeDtypeStruct((T, RWKV_WIDTH), F32)
    return pl.pallas_call(
        functools.partial(_premix_kernel, tiles_per_batch=tiles_per_batch, has_vres=has_vres),
        grid=(T // TM,),
        in_specs=in_specs,
        out_specs=[tok(RWKV_WIDTH)] * 8,
        out_shape=[out] * 8,
        scratch_shapes=[pltpu.VMEM((1, NC), F32)],
        compiler_params=pltpu.CompilerParams(dimension_semantics=("arbitrary",),
                                             vmem_limit_bytes=VMEM_LIMIT),
        name="premix",
    )(*args)


def _rwkv_kernel(r_ref, k_ref, v_ref, lw_ref, kk_ref, a_ref, g_ref, rk_ref, lng_ref, lnb_ref,
                 o_ref, state_ref, y_ref, ra_ref, yu_ref, xbk_ref, xv_ref, gc_ref):
    @pl.when(pl.program_id(1) == 0)
    def _():
        state_ref[...] = jnp.zeros_like(state_ref)

    ts = r_ref.shape[0]
    n_pairs = r_ref.shape[1] // PAIR
    n_chunks = ts // CHUNK
    ti = lax.broadcasted_iota(jnp.int32, (CHUNK, CHUNK), 0)
    si = lax.broadcasted_iota(jnp.int32, (CHUNK, CHUNK), 1)
    tril = (ti >= si).astype(BF16)
    ri = lax.broadcasted_iota(jnp.int32, (2 * PAIR, 2 * PAIR), 0)
    ci = lax.broadcasted_iota(jnp.int32, (2 * PAIR, 2 * PAIR), 1)
    keep = (ri % CHUNK) + (ri >= PAIR).astype(jnp.int32) > (ci % CHUNK)
    ei = lax.broadcasted_iota(jnp.int32, (PAIR, PAIR), 0)
    ej = lax.broadcasted_iota(jnp.int32, (PAIR, PAIR), 1)
    eye = (ei == ej).astype(F32)
    lane = lax.broadcasted_iota(jnp.int32, (CHUNK, PAIR), 1)
    head0 = lane < HEAD_SIZE
    zeros_pp = jnp.zeros((PAIR, PAIR), BF16)

    def stack(t):
        return jnp.concatenate([jnp.where(head0, t, 0.0), jnp.where(head0, 0.0, t)], axis=0).astype(BF16)

    def pre_body(it, carry):
        chains = []
        for cc in range(RWKV_CHUNKS_PER_ITER):
            c = it * RWKV_CHUNKS_PER_ITER + cc
            rows = pl.ds(pl.multiple_of(c * CHUNK, CHUNK), CHUNK)
            lw = lw_ref[rows, :]
            cum = _dot_split(tril, lw, 3, split_lhs=False)
            g_in = jnp.exp(cum)
            g_ex = jnp.exp(cum - lw)
            g_inv = jnp.exp(-cum)
            gc_ref[c] = g_in[CHUNK - 1:CHUNK, :]
            kk = kk_ref[rows, :]
            rt = r_ref[rows, :] * g_in
            kt = k_ref[rows, :] * g_inv
            bt = (kk * a_ref[rows, :]) * g_inv
            at = -kk * g_ex
            vv = v_ref[rows, :]
            for p in range(n_pairs):
                ls = slice(p * PAIR, (p + 1) * PAIR)
                xa, xb, xk, xr, xv = (stack(t[:, ls]) for t in (at, bt, kt, rt, vv))
                xbk = jnp.concatenate([xb, xk], axis=0)
                xbk_ref[c, p] = xbk
                xv_ref[c, p] = xv
                chains.append(dict(c=c, p=p, xa=xa, xr=xr, xv=xv, xbk=xbk))
        for ch in chains:
            big = _dot(jnp.concatenate([ch["xa"], ch["xr"]], axis=0), ch["xbk"], _NT)
            big = jnp.where(keep, big, 0.0)
            ch["lab"] = big[0:PAIR, 0:PAIR]
            ch["lak"] = big[0:PAIR, PAIR:2 * PAIR].astype(BF16)
            ch["lrbk"] = big[PAIR:2 * PAIR, :].astype(BF16)
        for ch in chains:
            lb = ch["lab"].astype(BF16)
            ch["pw"] = _dot(lb, lb)
            ch["lakv"] = _dot(ch["lak"], ch["xv"]).astype(BF16)
            ch["inv"] = eye + ch["lab"]
        for i in range(1, 6):
            for ch in chains:
                pb = ch["pw"].astype(BF16)
                sb = ch["inv"].astype(BF16)
                if i < 5:
                    res = _dot(pb, jnp.concatenate([pb, sb], axis=1))
                    ch["pw"] = res[:, 0:PAIR]
                    ch["inv"] = ch["inv"] + res[:, PAIR:2 * PAIR]
                else:
                    ch["inv"] = ch["inv"] + _dot(pb, sb)
        for ch in chains:
            au = _dot(ch["inv"].astype(BF16), jnp.concatenate([ch["xa"], ch["lakv"]], axis=1))
            ch["a2"] = au[:, 0:PAIR].astype(BF16)
            ch["uv"] = au[:, PAIR:2 * PAIR]
        for ch in chains:
            rhs = jnp.concatenate(
                [jnp.concatenate([ch["a2"], ch["uv"].astype(BF16)], axis=1),
                 jnp.concatenate([zeros_pp, ch["xv"]], axis=1)], axis=0)
            ry = _dot(ch["lrbk"], rhs)
            r2 = ch["xr"].astype(F32) + ry[:, 0:PAIR]
            ra_ref[ch["c"], ch["p"]] = jnp.concatenate([r2.astype(BF16), ch["a2"]], axis=0)
            yu_ref[ch["c"], ch["p"]] = jnp.concatenate([ry[:, PAIR:2 * PAIR], ch["uv"]], axis=0)
        return carry

    lax.fori_loop(0, n_chunks // RWKV_CHUNKS_PER_ITER, pre_body, 0)

    def seq_body(c, carry):
        rows = pl.ds(pl.multiple_of(c * CHUNK, CHUNK), CHUNK)
        gc = gc_ref[c]
        res = [_dot(ra_ref[c, p], state_ref[p].astype(BF16), _NT) + yu_ref[c, p] for p in range(n_pairs)]
        for p in range(n_pairs):
            ls = slice(p * PAIR, (p + 1) * PAIR)
            y_st = res[p][0:PAIR]
            u_st = res[p][PAIR:2 * PAIR].astype(BF16)
            st = _dot(jnp.concatenate([u_st, xv_ref[c, p]], axis=0), xbk_ref[c, p], _TN)
            state_ref[p] = (state_ref[p] + st) * gc[:, ls]
            y_ref[rows, ls] = y_st[0:CHUNK] + y_st[CHUNK:2 * CHUNK]
        return carry

    lax.fori_loop(0, n_chunks, seq_body, 0)

    ones = _group_ones(RWKV_WIDTH, HEAD_SIZE).astype(BF16)
    y = y_ref[...]
    mean = _dot_split(y, ones, 2) * (1.0 / HEAD_SIZE)
    d = y - mean
    var = _dot_split(d * d, ones, 2) * (1.0 / HEAD_SIZE)
    yn = d * lax.rsqrt(var + GN_EPS) * lng_ref[...] + lnb_ref[...]
    bonus = _dot_split(r_ref[...] * k_ref[...] * rk_ref[...], ones, 2) * v_ref[...]
    o_ref[...] = (yn + bonus) * g_ref[...]


def _rwkv_call(r, k, v, lw, kk, a, g, r_k, lnx_g, lnx_b, batch):
    T, W = r.shape
    steps = T // batch // TS
    n_chunks, n_pairs = TS // CHUNK, W // PAIR
    tok = pl.BlockSpec((TS, W), lambda b, j: (b * steps + j, 0))
    row = pl.BlockSpec((1, W), lambda b, j: (0, 0))
    return pl.pallas_call(
        _rwkv_kernel,
        grid=(batch, steps),
        in_specs=[tok] * 7 + [row] * 3,
        out_specs=tok,
        out_shape=jax.ShapeDtypeStruct((T, W), F32),
        scratch_shapes=[pltpu.VMEM((n_pairs, PAIR, PAIR), F32), pltpu.VMEM((TS, W), F32),
                        pltpu.VMEM((n_chunks, n_pairs, 2 * PAIR, PAIR), BF16),
                        pltpu.VMEM((n_chunks, n_pairs, 2 * PAIR, PAIR), F32),
                        pltpu.VMEM((n_chunks, n_pairs, 2 * PAIR, PAIR), BF16),
                        pltpu.VMEM((n_chunks, n_pairs, PAIR, PAIR), BF16),
                        pltpu.VMEM((n_chunks, 1, W), F32)],
        compiler_params=pltpu.CompilerParams(dimension_semantics=("arbitrary", "arbitrary"),
                                             vmem_limit_bytes=VMEM_LIMIT),
        name="rwkv",
    )(r, k, v, lw, kk, a, g, r_k, lnx_g, lnx_b)


def _postmix_kernel(x_ref, y_ref, p_ref, mod_ref, pw_ref, ps_ref, wo_ref, ln_ref,
                    x_out, ht_out, carry_ref, *, tiles_per_batch):
    i = pl.program_id(0)
    j = i % tiles_per_batch

    @pl.when(j == 0)
    def _():
        carry_ref[...] = jnp.zeros_like(carry_ref)

    m = mod_ref[0]
    p = p_ref[...]
    tm = p.shape[0]
    ext = jnp.concatenate([carry_ref[...], p], axis=0)
    carry_ref[...] = p[tm - MAX_WINDOW:tm, :]
    pos = (j * tm + 1 + lax.broadcasted_iota(jnp.int32, (tm, 1), 0)).astype(F32)
    pooled = []
    for gi, win in enumerate(POOL_WINDOWS):
        ls = slice(gi * POOL_GROUP, (gi + 1) * POOL_GROUP)
        wsum = ext[:, ls]
        span = 1
        while span < win:
            wsum = wsum + pltpu.roll(wsum, span, 0)
            span *= 2
        mean = wsum[MAX_WINDOW:, :] / jnp.minimum(pos, float(win))
        d = mean - p[:, ls]
        pooled.append(_dot3(d, pw_ref[gi]))
    y_pool = jnp.concatenate(pooled, axis=1) * ps_ref[...]
    mix = (_dot(y_ref[...].astype(BF16), wo_ref[0:RWKV_WIDTH, :])
           + _dot(y_pool.astype(BF16), wo_ref[RWKV_WIDTH:RWKV_WIDTH + POOL_WIDTH, :]))
    x = x_ref[...] + m[2:3] * mix
    x_out[...] = x
    h2 = _rms_mod(x, ln_ref[...], m[3:4], m[4:5])
    ht_out[...] = h2.T.astype(BF16)


def _postmix_call(x2, y_rwkv, pool_in, modl, pool_w, pool_scale, w_out, ln_g, tiles_per_batch):
    T, D = x2.shape
    row = lambda n: pl.BlockSpec((1, n), lambda i: (0, 0))
    full = lambda a: pl.BlockSpec(a.shape, lambda i: (0,) * a.ndim)
    tok = lambda n: pl.BlockSpec((TM, n), lambda i: (i, 0))
    return pl.pallas_call(
        functools.partial(_postmix_kernel, tiles_per_batch=tiles_per_batch),
        grid=(T // TM,),
        in_specs=[tok(D), tok(RWKV_WIDTH), tok(POOL_WIDTH),
                  pl.BlockSpec((1, 6, D), lambda i: (i // tiles_per_batch, 0, 0)),
                  full(pool_w), row(POOL_WIDTH), full(w_out), row(D)],
        out_specs=[tok(D), pl.BlockSpec((D, TM), lambda i: (0, i))],
        out_shape=[jax.ShapeDtypeStruct((T, D), F32), jax.ShapeDtypeStruct((D, T), BF16)],
        scratch_shapes=[pltpu.VMEM((MAX_WINDOW, POOL_WIDTH), F32)],
        compiler_params=pltpu.CompilerParams(dimension_semantics=("arbitrary",),
                                             vmem_limit_bytes=VMEM_LIMIT),
        name="postmix",
    )(x2, y_rwkv, pool_in, modl, pool_w, pool_scale, w_out, ln_g)


def _extract16(cur, mark, marks, tie_break):
    n, w = cur.shape
    iota = lax.broadcasted_iota(jnp.int32, (n, w), 0).astype(F32)
    vals = []
    for j in range(PEER_TOPK):
        mx = jnp.max(cur, axis=0, keepdims=True)
        hit = cur == mx
        if tie_break:
            first = jnp.min(jnp.where(hit, iota, float(n)), axis=0, keepdims=True)
            hit = iota == first
        mark = jnp.where(hit, marks[j], mark)
        cur = jnp.where(hit, -jnp.inf, cur)
        vals.append(mx)
    removed = jnp.sum((cur == -jnp.inf).astype(F32), axis=0, keepdims=True)
    return mark, vals, removed


def _pair_stage(vals1, v2_ref, cand_ref, bm_ref, tie_break):
    w = v2_ref.shape[1]
    for (a, nb), off in zip(_CAND_ROWS, _CAND_OFFS):
        cand_ref[off:off + nb, :] = vals1[a] + v2_ref[0:nb, :]
    cand_ref[N_CAND:N_CAND_PAD, :] = jnp.full((N_CAND_PAD - N_CAND, w), -jnp.inf, F32)
    cand = cand_ref[...]
    chosen, _, removed = _extract16(cand, jnp.zeros(cand.shape, F32), [1.0] * PEER_TOPK, tie_break)
    z = jnp.sum(chosen * jnp.exp(cand - cand[0:1, :]), axis=0, keepdims=True)
    cand_ref[...] = chosen
    for (a, nb), off in zip(_CAND_ROWS, _CAND_OFFS):
        bm_ref[a:a + 1, :] = jnp.sum(cand_ref[off:off + nb, :], axis=0, keepdims=True)
    return z, removed - float(N_CAND_PAD - N_CAND)


def _select_strip_exact(s1, s2, outs, lanes, v2_ref, cand_ref, bm_ref):
    rank2_out, e2_out, b_out, f_out = outs
    ranks = [float(j + 1) for j in range(PEER_TOPK)]
    unranked = jnp.full(s1.shape, 99.0, F32)
    rank1, vals1, _ = _extract16(s1, unranked, ranks, True)
    rank2, vals2, _ = _extract16(s2, unranked, ranks, True)
    for j in range(PEER_TOPK):
        v2_ref[j:j + 1, :] = vals2[j]
    z, _ = _pair_stage(vals1, v2_ref, cand_ref, bm_ref, True)
    bsel = jnp.zeros(s1.shape, F32)
    for a in range(PEER_TOPK):
        bsel = jnp.where(rank1 == ranks[a], bm_ref[a:a + 1, :], bsel)
    rank2_out[0, :, lanes] = rank2.astype(BF16)
    e2_out[0, :, lanes] = jnp.exp(s2 - vals2[0]).astype(BF16)
    b_out[0, :, lanes] = bsel
    f_out[0, :, lanes] = jnp.exp(s1 - vals1[0]) / z


def _oddeven_merge(lo, hi, r):
    step = r * 2
    if step < hi - lo:
        yield from _oddeven_merge(lo, hi, step)
        yield from _oddeven_merge(lo + r, hi, step)
        yield from [(i, i + r) for i in range(lo + r, hi - r, step)]
    else:
        yield (lo, lo + r)


def _oddeven_sort(lo, hi):
    if hi - lo >= 1:
        mid = lo + (hi - lo) // 2
        yield from _oddeven_sort(lo, mid)
        yield from _oddeven_sort(mid + 1, hi)
        yield from _oddeven_merge(lo, hi, 1)


_SORT16 = tuple(_oddeven_sort(0, PEER_TOPK - 1))


def _exchange(a, i, j):
    a[i], a[j] = jnp.maximum(a[i], a[j]), jnp.minimum(a[i], a[j])


def _sorted_top16(slabs):
    a = list(slabs)
    for i, j in _SORT16:
        _exchange(a, i, j)
    for shift in (4, 2, 1):
        other = [pltpu.roll(t, shift, 0) for t in a]
        a = [jnp.maximum(a[j], other[PEER_TOPK - 1 - j]) for j in range(PEER_TOPK)]
        d = PEER_TOPK // 2
        while d >= 1:
            for i in range(PEER_TOPK):
                if i & d == 0:
                    _exchange(a, i, i + d)
            d //= 2
    return a


def _count_above(x, v):
    c1 = v[7] > x
    c2 = jnp.where(c1, v[11], v[3]) > x
    c3 = jnp.where(c1, jnp.where(c2, v[13], v[9]), jnp.where(c2, v[5], v[1])) > x
    c4 = jnp.where(c1,
                   jnp.where(c2, jnp.where(c3, v[14], v[12]), jnp.where(c3, v[10], v[8])),
                   jnp.where(c2, jnp.where(c3, v[6], v[4]), jnp.where(c3, v[2], v[0]))) > x
    return c1, c2, c3, c4


def _pick16(masks, table):
    t = list(table)
    for c in reversed(masks):
        t = [jnp.where(c, t[2 * k + 1], t[2 * k]) for k in range(len(t) // 2)]
    return t[0]


def _select_strip_fast(s1, s2, outs, lanes, v2_ref, cand_ref, bm_ref):
    rank2_out, e2_out, b_out, f_out = outs
    w = s1.shape[1]
    n_slabs = s1.shape[0] // 8
    slabs1 = [s1[8 * k:8 * k + 8, :] for k in range(n_slabs)]
    slabs2 = [s2[8 * k:8 * k + 8, :] for k in range(n_slabs)]
    top1 = _sorted_top16(slabs1)
    top2 = _sorted_top16(slabs2)
    for j in range(PEER_TOPK):
        v2_ref[j:j + 1, :] = top2[j][0:1, :]
    z, removed = _pair_stage([t[0:1, :] for t in top1], v2_ref, cand_ref, bm_ref, False)
    counts = [jnp.broadcast_to(bm_ref[a:a + 1, :], (8, w)) for a in range(PEER_TOPK)]
    zb = jnp.broadcast_to(z, (8, w))
    suspect = jnp.zeros((8, w), F32)
    for j in range(PEER_TOPK - 1):
        suspect = jnp.where(top1[j] == top1[j + 1], 99.0, suspect)
        suspect = jnp.where(top2[j] == top2[j + 1], 99.0, suspect)
    in_top1 = jnp.zeros((8, w), F32)
    in_top2 = jnp.zeros((8, w), F32)
    rank2, e2, bsel, fsel = [], [], [], []
    for k in range(n_slabs):
        x1, x2 = slabs1[k], slabs2[k]
        keep1 = x1 >= top1[PEER_TOPK - 1]
        keep2 = x2 >= top2[PEER_TOPK - 1]
        in_top1 = in_top1 + jnp.where(keep1, 1.0, 0.0)
        in_top2 = in_top2 + jnp.where(keep2, 1.0, 0.0)
        bsel.append(jnp.where(keep1, _pick16(_count_above(x1, top1), counts), 0.0))
        c1, c2, c3, c4 = _count_above(x2, top2)
        rank = (1.0 + jnp.where(c1, 8.0, 0.0)) + (jnp.where(c2, 4.0, 0.0)
                                                  + (jnp.where(c3, 2.0, 0.0) + jnp.where(c4, 1.0, 0.0)))
        rank2.append(jnp.where(keep2, rank, 99.0))
        e2.append(jnp.exp(x2 - top2[0]))
        fsel.append(jnp.exp(x1 - top1[0]) / zb)
    rank2_out[0, :, lanes] = jnp.concatenate(rank2, axis=0).astype(BF16)
    e2_out[0, :, lanes] = jnp.concatenate(e2, axis=0).astype(BF16)
    b_out[0, :, lanes] = jnp.concatenate(bsel, axis=0)
    f_out[0, :, lanes] = jnp.concatenate(fsel, axis=0)
    kept = jnp.maximum(jnp.sum(in_top1, axis=0, keepdims=True), jnp.sum(in_top2, axis=0, keepdims=True))
    return jnp.maximum(jnp.max(jnp.maximum(kept, removed)), jnp.max(suspect))


def _keyproj_kernel(keys_ref, q_ref, o_ref):
    o_ref[...] = _dot(keys_ref[0], q_ref[...], _NT, precision=HIGHEST).astype(BF16)


def _keyproj_call(keys, peer_q):
    D = peer_q.shape[0]
    n = keys.shape[0] * keys.shape[1]
    return pl.pallas_call(
        _keyproj_kernel,
        grid=(n,),
        in_specs=[pl.BlockSpec((1, N_KEYS, N_KEYS), lambda i: (i, 0, 0)),
                  pl.BlockSpec((D, N_KEYS), lambda i: (0, i))],
        out_specs=pl.BlockSpec((N_KEYS, D), lambda i: (i, 0)),
        out_shape=jax.ShapeDtypeStruct((n * N_KEYS, D), BF16),
        name="keyproj",
    )(keys.reshape(n, N_KEYS, N_KEYS), peer_q)


def _select_kernel(ht_ref, ws_ref, rank2_out, e2_out, b_out, f_out,
                   s1_ref, s2_ref, v2_ref, cand_ref, bm_ref):
    s = _dot(ws_ref[...], ht_ref[...])
    s1_ref[...] = s[0:N_KEYS]
    s2_ref[...] = s[N_KEYS:2 * N_KEYS]
    outs = (rank2_out, e2_out, b_out, f_out)

    def strip(j, carry):
        lanes = pl.ds(pl.multiple_of(j * SELECT_STRIP, SELECT_STRIP), SELECT_STRIP)
        s1 = s1_ref[:, lanes]
        s2 = s2_ref[:, lanes]
        maybe_tied = _select_strip_fast(s1, s2, outs, lanes, v2_ref, cand_ref, bm_ref)

        @pl.when(maybe_tied > float(PEER_TOPK))
        def _():
            _select_strip_exact(s1, s2, outs, lanes, v2_ref, cand_ref, bm_ref)

        return carry

    lax.fori_loop(0, s1_ref.shape[1] // SELECT_STRIP, strip, 0)


def _select_call(h2t, ws):
    D, T = h2t.shape
    out = jax.ShapeDtypeStruct((PEER_HEADS, N_KEYS, T), F32)
    out16 = jax.ShapeDtypeStruct((PEER_HEADS, N_KEYS, T), BF16)
    ospec = pl.BlockSpec((1, N_KEYS, TQ), lambda i, h: (h, 0, i))
    return pl.pallas_call(
        _select_kernel,
        grid=(T // TQ, PEER_HEADS),
        in_specs=[pl.BlockSpec((D, TQ), lambda i, h: (0, i)),
                  pl.BlockSpec((2 * N_KEYS, D), lambda i, h: (h, 0))],
        out_specs=[ospec] * 4,
        out_shape=[out16, out16, out, out],
        scratch_shapes=[pltpu.VMEM((N_KEYS, TQ), F32), pltpu.VMEM((N_KEYS, TQ), F32),
                        pltpu.VMEM((PEER_TOPK, SELECT_STRIP), F32),
                        pltpu.VMEM((N_CAND_PAD, SELECT_STRIP), F32),
                        pltpu.VMEM((PEER_TOPK, SELECT_STRIP), F32)],
        compiler_params=pltpu.CompilerParams(dimension_semantics=("arbitrary", "arbitrary"),
                                             vmem_limit_bytes=VMEM_LIMIT),
        name="select",
    )(h2t, ws)


def _experts_kernel(ht_ref, u_ref, vt_ref, rank2_ref, e2_ref, b_ref, f_ref, x_ref, mod_ref, lnf_ref,
                    x_out, acc_ref, *, tiles_per_batch, final):
    e = pl.program_id(1)

    @pl.when(e == 0)
    def _():
        acc_ref[...] = jnp.zeros_like(acc_ref)

    for sb in range(ht_ref.shape[1] // EXPERT_SUB):
        cols = slice(sb * EXPERT_SUB, (sb + 1) * EXPERT_SUB)
        gates = []
        for ii in range(FIRST_KEYS_PER_STEP):
            gate = None
            for h in range(PEER_HEADS):
                brow = jnp.broadcast_to(b_ref[h, e, ii:ii + 1, cols], (N_KEYS, EXPERT_SUB)).astype(BF16)
                frow = jnp.broadcast_to(f_ref[h, e, ii:ii + 1, cols], (N_KEYS, EXPERT_SUB)).astype(BF16)
                term = jnp.where(rank2_ref[h, :, cols] <= brow, e2_ref[h, :, cols], 0.0) * frow
                gate = term if gate is None else gate + term
            gates.append(gate)
        act = _dot(u_ref[...], ht_ref[:, cols]).astype(BF16)
        gelu = 0.5 * act * (1.0 + lax.erf(act * (2.0 ** -0.5)))
        z = jnp.concatenate(gates, axis=0) * gelu
        acc_ref[:, cols] += _dot(vt_ref[...], z)

    @pl.when(e == pl.num_programs(1) - 1)
    def _():
        m = mod_ref[0]
        x = x_ref[...] + m[5:6] * acc_ref[...].T
        if final:
            x = x * lax.rsqrt(jnp.mean(x * x, axis=-1, keepdims=True) + NORM_EPS) * lnf_ref[...]
        x_out[...] = x


def _experts_call(h2t, u_bf, vt_bf, rank2, e2, bsel, fsel, x2, modl, lnf_g, tiles_per_batch, final):
    D, T = h2t.shape
    NE = u_bf.shape[0]
    once = dict(pipeline_mode=pl.Buffered(1))
    sel = pl.BlockSpec((PEER_HEADS, N_KEYS, TT), lambda i, e: (0, 0, i), **once)
    bsel, fsel = (t.reshape(PEER_HEADS, N_KEYS // FIRST_KEYS_PER_STEP, FIRST_KEYS_PER_STEP, T)
                  for t in (bsel, fsel))
    sel1 = pl.BlockSpec(bsel.shape[:3] + (TT,), lambda i, e: (0, 0, 0, i), **once)
    tok = pl.BlockSpec((TT, D), lambda i, e: (i, 0))
    return pl.pallas_call(
        functools.partial(_experts_kernel, tiles_per_batch=tiles_per_batch, final=final),
        grid=(T // TT, NE // EB),
        in_specs=[pl.BlockSpec((D, TT), lambda i, e: (0, i), **once),
                  pl.BlockSpec((EB, D), lambda i, e: (e, 0)),
                  pl.BlockSpec((D, EB), lambda i, e: (0, e)),
                  sel, sel, sel1, sel1, pl.BlockSpec((TT, D), lambda i, e: (i, 0), **once),
                  pl.BlockSpec((1, 6, D), lambda i, e: (i // tiles_per_batch, 0, 0)),
                  pl.BlockSpec((1, D), lambda i, e: (0, 0))],
        out_specs=tok,
        out_shape=jax.ShapeDtypeStruct((T, D), F32),
        scratch_shapes=[pltpu.VMEM((D, TT), F32)],
        compiler_params=pltpu.CompilerParams(dimension_semantics=("arbitrary", "arbitrary"),
                                             vmem_limit_bytes=VMEM_LIMIT),
        name="experts",
    )(h2t, u_bf, vt_bf, rank2, e2, bsel, fsel, x2, modl, lnf_g)


def kernel(x, c, ada_w, ada_b, ln1_g, w_in, mu_shift, w0, w_up, a0, a_up, g_up, vres_down, vres_mu,
           vres_v0, vres_up, k_k, k_a, r_k, lnx_g, lnx_b, pool_w, pool_scale, w_out, ln2_g, peer_q,
           peer_keys, peer_u, peer_v, lnf_g):
    B, S, D = x.shape
    L = ada_w.shape[0]
    T = B * S
    assert S % TS == 0 and S % TM == 0 and T % TQ == 0 and T % TT == 0 and TT <= S
    row = lambda t: t.reshape(1, -1)

    c_pad = jnp.zeros((8, D), F32).at[:B].set(c)
    mod = _ada_call(c_pad, ada_w, ada_b)[:, :B].reshape(L, B, 6, D)

    half = LORA_COLS // 2
    x2 = x.reshape(T, D)
    v_first = None
    for l in range(L):
        if l == 0:
            w_full = w_in[l].astype(BF16)
            vres = None
        else:
            mv = vres_down.shape[-1]
            w_full = jnp.concatenate(
                [w_in[l], vres_down[l - 1], jnp.zeros((D, VRES_PAD - mv), F32)], axis=1).astype(BF16)
            vres = (jnp.zeros((1, VRES_PAD), F32).at[:, :mv].set(vres_mu[l - 1]),
                    row(vres_v0[l - 1]),
                    jnp.zeros((VRES_PAD, RWKV_WIDTH), F32).at[:mv].set(vres_up[l - 1]),
                    v_first)
        lora_w = jnp.zeros((LORA_COLS, 2 * RWKV_WIDTH), F32)
        lora_w = lora_w.at[:half, :RWKV_WIDTH].set(w_up[l]).at[half:, RWKV_WIDTH:].set(a_up[l])
        r, k2, v, lw, kk, a, g, pool_in = _premix_call(
            x2, mod[l], row(ln1_g[l]), w_full, row(mu_shift[l]), row(w0[l]), row(a0[l]), lora_w,
            g_up[l], row(k_k[l]), row(k_a[l]), vres, S // TM)
        if l == 0:
            v_first = v
        y_rwkv = _rwkv_call(r, k2, v, lw, kk, a, g, row(r_k[l]), row(lnx_g[l]), row(lnx_b[l]), B)
        x_mid, h2t = _postmix_call(x2, y_rwkv, pool_in, mod[l], pool_w[l], row(pool_scale[l]),
                                   w_out[l].astype(BF16), row(ln2_g[l]), S // TM)
        ws = _keyproj_call(peer_keys[l], peer_q[l])
        rank2, e2, bsel, fsel = _select_call(h2t, ws)
        x2 = _experts_call(h2t, peer_u[l].astype(BF16), peer_v[l].T.astype(BF16), rank2, e2, bsel, fsel,
                           x_mid, mod[l], row(lnf_g), S // TT, l == L - 1)
    return x2.reshape(B, S, D)
```

```python
import functools

import jax
import jax.numpy as jnp
from jax import lax
from jax.experimental import pallas as pl
from jax.experimental.pallas import tpu as pltpu

F32 = jnp.float32
BF16 = jnp.bfloat16
HIGHEST = lax.Precision.HIGHEST

HEAD_SIZE = 64
RWKV_WIDTH = 512
POOL_WIDTH = 512
POOL_GROUP = 128
POOL_WINDOWS = (2, 4, 8, 16)
MAX_WINDOW = 16
RWKV_COLS = 1792
LORA_COLS = 128
GATE_LORA = 128
VRES_PAD = 128
GN_EPS = 64e-5
NORM_EPS = 1e-6
N_KEYS = 128
PEER_HEADS = 8
PEER_TOPK = 16
CHUNK = 64
PAIR = 128
RWKV_SEQS_PER_STEP = 2

TM = 512
TS = 512
TQ = 512
SELECT_STRIP = 128
TT = 1024
EB = 1024
FIRST_KEYS_PER_STEP = EB // N_KEYS

VMEM_LIMIT = 56 * 1024 * 1024

_CAND_ROWS = [(a, PEER_TOPK // (a + 1)) for a in range(PEER_TOPK)]
_CAND_OFFS = []
_off = 0
for _a, _nb in _CAND_ROWS:
    _CAND_OFFS.append(_off)
    _off += _nb
N_CAND = _off
N_CAND_PAD = -(-N_CAND // 8) * 8


def _dot(a, b, dims=(((1,), (0,)), ((), ())), precision=None):
    return lax.dot_general(a, b, dims, precision=precision, preferred_element_type=F32)


_NT = (((1,), (1,)), ((), ()))
_TN = (((0,), (0,)), ((), ()))
_NN = (((1,), (0,)), ((), ()))


def _dot_split(a, b, pieces, split_lhs=True):
    rem = a if split_lhs else b
    acc = None
    for i in range(pieces):
        hi = rem.astype(BF16)
        part = _dot(hi, b, _NN) if split_lhs else _dot(a, hi, _NN)
        acc = part if acc is None else acc + part
        if i + 1 < pieces:
            rem = rem - hi.astype(F32)
    return acc


def _dot3(a, b):
    ah = a.astype(BF16)
    bh = b.astype(BF16)
    al = (a - ah.astype(F32)).astype(BF16)
    bl = (b - bh.astype(F32)).astype(BF16)
    return _dot(ah, bh) + (_dot(ah, bl) + _dot(al, bh))


def _group_ones(n, group):
    r = lax.broadcasted_iota(jnp.int32, (n, n), 0) // group
    c = lax.broadcasted_iota(jnp.int32, (n, n), 1) // group
    return (r == c).astype(F32)


def _rms_mod(x, g, shift, scale):
    y = x * lax.rsqrt(jnp.mean(x * x, axis=-1, keepdims=True) + NORM_EPS)
    return (y * g) * (1.0 + scale) + shift


def _ada_kernel(c_ref, w_ref, b_ref, o_ref):
    c = c_ref[...]
    ca = c * jax.nn.sigmoid(c)
    o_ref[0] = _dot(ca, w_ref[0], precision=HIGHEST) + b_ref[0]


def _ada_call(c_pad, ada_w, ada_b):
    L, D, D6 = ada_w.shape
    nb = D6 // D
    return pl.pallas_call(
        _ada_kernel,
        grid=(L, nb),
        in_specs=[
            pl.BlockSpec((8, D), lambda l, j: (0, 0)),
            pl.BlockSpec((1, D, D), lambda l, j: (l, 0, j)),
            pl.BlockSpec((1, 1, D), lambda l, j: (l, 0, j)),
        ],
        out_specs=pl.BlockSpec((1, 8, D), lambda l, j: (l, 0, j)),
        out_shape=jax.ShapeDtypeStruct((L, 8, D6), F32),
        name="adaln",
    )(c_pad, ada_w, ada_b.reshape(L, 1, D6))


def _premix_kernel(*refs, tiles_per_batch, has_vres):
    if has_vres:
        (x_ref, mod_ref, ln_ref, w_ref, mu_ref, w0_ref, a0_ref, lora_ref, gup_ref, kk_ref, ka_ref,
         vmu_ref, v0_ref, vup_ref, vfirst_ref,
         r_out, k_out, v_out, lw_out, kkn_out, a_out, g_out, p_out, carry_ref) = refs
    else:
        (x_ref, mod_ref, ln_ref, w_ref, mu_ref, w0_ref, a0_ref, lora_ref, gup_ref, kk_ref, ka_ref,
         r_out, k_out, v_out, lw_out, kkn_out, a_out, g_out, p_out, carry_ref) = refs
    i = pl.program_id(0)

    @pl.when(i % tiles_per_batch == 0)
    def _():
        carry_ref[...] = jnp.zeros_like(carry_ref)

    m = mod_ref[0]
    h = _rms_mod(x_ref[...], ln_ref[...], m[0:1], m[1:2])
    proj = _dot(h.astype(BF16), w_ref[...])
    tm = proj.shape[0]
    row0 = lax.broadcasted_iota(jnp.int32, (tm, 1), 0) == 0
    carry = carry_ref[...]
    carry_ref[...] = proj[tm - 1:tm, :]

    def shifted(lo, hi, mu):
        z = proj[:, lo:hi]
        prev = jnp.where(row0, carry[:, lo:hi], pltpu.roll(z, 1, 0))
        return z + (prev - z) * mu

    rw = shifted(0, RWKV_COLS, mu_ref[...])
    r = rw[:, 0:RWKV_WIDTH]
    k = rw[:, RWKV_WIDTH:2 * RWKV_WIDTH]
    v = rw[:, 2 * RWKV_WIDTH:3 * RWKV_WIDTH]
    lo = 3 * RWKV_WIDTH
    wa = rw[:, lo:lo + LORA_COLS]
    gd = rw[:, lo + LORA_COLS:lo + LORA_COLS + GATE_LORA]
    lane = lax.broadcasted_iota(jnp.int32, wa.shape, 1)
    wa = jnp.where(lane < LORA_COLS // 2, jnp.tanh(wa), wa)
    lora = _dot3(wa, lora_ref[...])
    zw = -(w0_ref[...] + lora[:, 0:RWKV_WIDTH])
    softplus = jnp.maximum(zw, 0.0) + jnp.log1p(jnp.exp(-jnp.abs(zw)))
    w = -softplus - 0.5
    a = jax.nn.sigmoid(a0_ref[...] + lora[:, RWKV_WIDTH:2 * RWKV_WIDTH])
    g = _dot3(jax.nn.sigmoid(gd), gup_ref[...])
    kk = k * kk_ref[...]
    ss = _dot_split(kk * kk, _group_ones(RWKV_WIDTH, HEAD_SIZE).astype(BF16), 3)
    kk = kk / jnp.maximum(jnp.sqrt(ss), 1e-12)
    k2 = k * (1.0 + (a - 1.0) * ka_ref[...])
    if has_vres:
        base = RWKV_COLS + POOL_WIDTH
        vd = shifted(base, base + VRES_PAD, vmu_ref[...])
        mixv = jax.nn.sigmoid(v0_ref[...] + _dot3(vd, vup_ref[...]))
        v = v + (vfirst_ref[...] - v) * mixv
    r_out[...] = r
    k_out[...] = k2
    v_out[...] = v
    lw_out[...] = -jnp.exp(w)
    kkn_out[...] = kk
    a_out[...] = a
    g_out[...] = g
    p_out[...] = proj[:, RWKV_COLS:RWKV_COLS + POOL_WIDTH]


def _premix_call(x2, modl, ln_g, w_full, mu, w0, a0, lora_w, g_up, k_k, k_a, vres, tiles_per_batch):
    T, D = x2.shape
    NC = w_full.shape[1]
    has_vres = vres is not None
    row = lambda n: pl.BlockSpec((1, n), lambda i: (0, 0))
    full = lambda a: pl.BlockSpec(a.shape, lambda i: (0,) * a.ndim)
    tok = lambda n: pl.BlockSpec((TM, n), lambda i: (i, 0))
    in_specs = [tok(D), pl.BlockSpec((1, 6, D), lambda i: (i // tiles_per_batch, 0, 0)), row(D),
                full(w_full), row(RWKV_COLS), row(RWKV_WIDTH), row(RWKV_WIDTH), full(lora_w), full(g_up),
                row(RWKV_WIDTH), row(RWKV_WIDTH)]
    args = [x2, modl, ln_g, w_full, mu, w0, a0, lora_w, g_up, k_k, k_a]
    if has_vres:
        vmu, v0, vup, vfirst = vres
        in_specs += [row(VRES_PAD), row(RWKV_WIDTH), full(vup), tok(RWKV_WIDTH)]
        args += [vmu, v0, vup, vfirst]
    out = jax.ShapeDtypeStruct((T, RWKV_WIDTH), F32)
    return pl.pallas_call(
        functools.partial(_premix_kernel, tiles_per_batch=tiles_per_batch, has_vres=has_vres),
        grid=(T // TM,),
        in_specs=in_specs,
        out_specs=[tok(RWKV_WIDTH)] * 8,
        out_shape=[out] * 8,
        scratch_shapes=[pltpu.VMEM((1, NC), F32)],
        compiler_params=pltpu.CompilerParams(dimension_semantics=("arbitrary",),
                                             vmem_limit_bytes=VMEM_LIMIT),
        name="premix",
    )(*args)


def _rwkv_kernel(r_ref, k_ref, v_ref, lw_ref, kk_ref, a_ref, g_ref, rk_ref, lng_ref, lnb_ref,
                 o_ref, state_ref, y_ref, ra_ref, yu_ref, xbk_ref, xv_ref, gc_ref):
    @pl.when(pl.program_id(1) == 0)
    def _():
        state_ref[...] = jnp.zeros_like(state_ref)

    n_seq, ts, width = r_ref.shape
    n_pairs = width // PAIR
    n_chunks = ts // CHUNK
    ti = lax.broadcasted_iota(jnp.int32, (CHUNK, CHUNK), 0)
    si = lax.broadcasted_iota(jnp.int32, (CHUNK, CHUNK), 1)
    tril = (ti >= si).astype(BF16)
    ri = lax.broadcasted_iota(jnp.int32, (2 * PAIR, 2 * PAIR), 0)
    ci = lax.broadcasted_iota(jnp.int32, (2 * PAIR, 2 * PAIR), 1)
    keep = (ri % CHUNK) + (ri >= PAIR).astype(jnp.int32) > (ci % CHUNK)
    ei = lax.broadcasted_iota(jnp.int32, (PAIR, PAIR), 0)
    ej = lax.broadcasted_iota(jnp.int32, (PAIR, PAIR), 1)
    eye = (ei == ej).astype(F32)
    lane = lax.broadcasted_iota(jnp.int32, (CHUNK, PAIR), 1)
    head0 = lane < HEAD_SIZE
    zeros_pp = jnp.zeros((PAIR, PAIR), BF16)

    def stack(t):
        return jnp.concatenate([jnp.where(head0, t, 0.0), jnp.where(head0, 0.0, t)], axis=0).astype(BF16)

    def pre_body(b, carry):
        chains = []
        for cc in range(n_chunks):
            c = b * n_chunks + cc
            rows = slice(cc * CHUNK, (cc + 1) * CHUNK)
            lw = lw_ref[b, rows, :]
            cum = _dot_split(tril, lw, 3, split_lhs=False)
            g_in = jnp.exp(cum)
            g_ex = jnp.exp(cum - lw)
            g_inv = jnp.exp(-cum)
            gc_ref[c] = g_in[CHUNK - 1:CHUNK, :]
            kk = kk_ref[b, rows, :]
            rt = r_ref[b, rows, :] * g_in
            kt = k_ref[b, rows, :] * g_inv
            bt = (kk * a_ref[b, rows, :]) * g_inv
            at = -kk * g_ex
            vv = v_ref[b, rows, :]
            for p in range(n_pairs):
                ls = slice(p * PAIR, (p + 1) * PAIR)
                xa, xb, xk, xr, xv = (stack(t[:, ls]) for t in (at, bt, kt, rt, vv))
                xbk = jnp.concatenate([xb, xk], axis=0)
                xbk_ref[c, p] = xbk
                xv_ref[c, p] = xv
                chains.append(dict(c=c, p=p, xa=xa, xr=xr, xv=xv, xbk=xbk))
        for ch in chains:
            big = _dot(jnp.concatenate([ch["xa"], ch["xr"]], axis=0), ch["xbk"], _NT)
            big = jnp.where(keep, big, 0.0)
            ch["lab"] = big[0:PAIR, 0:PAIR]
            ch["lak"] = big[0:PAIR, PAIR:2 * PAIR].astype(BF16)
            ch["lrbk"] = big[PAIR:2 * PAIR, :].astype(BF16)
        for ch in chains:
            lb = ch["lab"].astype(BF16)
            ch["pw"] = _dot(lb, lb)
            ch["lakv"] = _dot(ch["lak"], ch["xv"]).astype(BF16)
            ch["inv"] = eye + ch["lab"]
        for i in range(1, 6):
            for ch in chains:
                pb = ch["pw"].astype(BF16)
                sb = ch["inv"].astype(BF16)
                if i < 5:
                    res = _dot(pb, jnp.concatenate([pb, sb], axis=1))
                    ch["pw"] = res[:, 0:PAIR]
                    ch["inv"] = ch["inv"] + res[:, PAIR:2 * PAIR]
                else:
                    ch["inv"] = ch["inv"] + _dot(pb, sb)
        for ch in chains:
            au = _dot(ch["inv"].astype(BF16), jnp.concatenate([ch["xa"], ch["lakv"]], axis=1))
            ch["a2"] = au[:, 0:PAIR].astype(BF16)
            ch["uv"] = au[:, PAIR:2 * PAIR]
        for ch in chains:
            rhs = jnp.concatenate(
                [jnp.concatenate([ch["a2"], ch["uv"].astype(BF16)], axis=1),
                 jnp.concatenate([zeros_pp, ch["xv"]], axis=1)], axis=0)
            ry = _dot(ch["lrbk"], rhs)
            r2 = ch["xr"].astype(F32) + ry[:, 0:PAIR]
            ra_ref[ch["c"], ch["p"]] = jnp.concatenate([r2.astype(BF16), ch["a2"]], axis=0)
            yu_ref[ch["c"], ch["p"]] = jnp.concatenate([ry[:, PAIR:2 * PAIR], ch["uv"]], axis=0)
        return carry

    lax.fori_loop(0, n_seq, pre_body, 0)

    def seq_body(cc, carry):
        rows = pl.ds(pl.multiple_of(cc * CHUNK, CHUNK), CHUNK)
        chains = [(b, p, b * n_chunks + cc) for b in range(n_seq) for p in range(n_pairs)]
        res = [_dot(ra_ref[c, p], state_ref[b, p].astype(BF16), _NT) + yu_ref[c, p] for b, p, c in chains]
        for (b, p, c), res_bp in zip(chains, res):
            ls = slice(p * PAIR, (p + 1) * PAIR)
            y_st = res_bp[0:PAIR]
            u_st = res_bp[PAIR:2 * PAIR].astype(BF16)
            st = _dot(jnp.concatenate([u_st, xv_ref[c, p]], axis=0), xbk_ref[c, p], _TN)
            state_ref[b, p] = (state_ref[b, p] + st) * gc_ref[c][:, ls]
            y_ref[b, rows, ls] = y_st[0:CHUNK] + y_st[CHUNK:2 * CHUNK]
        return carry

    lax.fori_loop(0, n_chunks, seq_body, 0)

    ones = _group_ones(RWKV_WIDTH, HEAD_SIZE).astype(BF16)
    flat = lambda ref: ref[...].reshape(n_seq * ts, width)
    y = flat(y_ref)
    mean = _dot_split(y, ones, 2) * (1.0 / HEAD_SIZE)
    d = y - mean
    var = _dot_split(d * d, ones, 2) * (1.0 / HEAD_SIZE)
    yn = d * lax.rsqrt(var + GN_EPS) * lng_ref[...] + lnb_ref[...]
    bonus = _dot_split(flat(r_ref) * flat(k_ref) * rk_ref[...], ones, 2) * flat(v_ref)
    o_ref[...] = ((yn + bonus) * flat(g_ref)).reshape(n_seq, ts, width)


def _rwkv_call(r, k, v, lw, kk, a, g, r_k, lnx_g, lnx_b, batch):
    T, W = r.shape
    seq = T // batch
    ts = TS // RWKV_SEQS_PER_STEP
    n_chunks, n_pairs = TS // CHUNK, W // PAIR
    tok = pl.BlockSpec((RWKV_SEQS_PER_STEP, ts, W), lambda b, j: (b, j, 0))
    row = pl.BlockSpec((1, W), lambda b, j: (0, 0))
    per_seq = lambda t: t.reshape(batch, seq, W)
    return pl.pallas_call(
        _rwkv_kernel,
        grid=(batch // RWKV_SEQS_PER_STEP, seq // ts),
        in_specs=[tok] * 7 + [row] * 3,
        out_specs=tok,
        out_shape=jax.ShapeDtypeStruct((batch, seq, W), F32),
        scratch_shapes=[pltpu.VMEM((RWKV_SEQS_PER_STEP, n_pairs, PAIR, PAIR), F32),
                        pltpu.VMEM((RWKV_SEQS_PER_STEP, ts, W), F32),
                        pltpu.VMEM((n_chunks, n_pairs, 2 * PAIR, PAIR), BF16),
                        pltpu.VMEM((n_chunks, n_pairs, 2 * PAIR, PAIR), F32),
                        pltpu.VMEM((n_chunks, n_pairs, 2 * PAIR, PAIR), BF16),
                        pltpu.VMEM((n_chunks, n_pairs, PAIR, PAIR), BF16),
                        pltpu.VMEM((n_chunks, 1, W), F32)],
        compiler_params=pltpu.CompilerParams(dimension_semantics=("arbitrary", "arbitrary"),
                                             vmem_limit_bytes=VMEM_LIMIT),
        name="rwkv",
    )(*(per_seq(t) for t in (r, k, v, lw, kk, a, g)), r_k, lnx_g, lnx_b).reshape(T, W)


def _postmix_kernel(x_ref, y_ref, p_ref, mod_ref, pw_ref, ps_ref, wo_ref, ln_ref,
                    x_out, ht_out, carry_ref, *, tiles_per_batch):
    i = pl.program_id(0)
    j = i % tiles_per_batch

    @pl.when(j == 0)
    def _():
        carry_ref[...] = jnp.zeros_like(carry_ref)

    m = mod_ref[0]
    p = p_ref[...]
    tm = p.shape[0]
    ext = jnp.concatenate([carry_ref[...], p], axis=0)
    carry_ref[...] = p[tm - MAX_WINDOW:tm, :]
    pos = (j * tm + 1 + lax.broadcasted_iota(jnp.int32, (tm, 1), 0)).astype(F32)
    pooled = []
    for gi, win in enumerate(POOL_WINDOWS):
        ls = slice(gi * POOL_GROUP, (gi + 1) * POOL_GROUP)
        wsum = ext[:, ls]
        span = 1
        while span < win:
            wsum = wsum + pltpu.roll(wsum, span, 0)
            span *= 2
        mean = wsum[MAX_WINDOW:, :] / jnp.minimum(pos, float(win))
        d = mean - p[:, ls]
        pooled.append(_dot3(d, pw_ref[gi]))
    y_pool = jnp.concatenate(pooled, axis=1) * ps_ref[...]
    mix = (_dot(y_ref[...].astype(BF16), wo_ref[0:RWKV_WIDTH, :])
           + _dot(y_pool.astype(BF16), wo_ref[RWKV_WIDTH:RWKV_WIDTH + POOL_WIDTH, :]))
    x = x_ref[...] + m[2:3] * mix
    x_out[...] = x
    h2 = _rms_mod(x, ln_ref[...], m[3:4], m[4:5])
    ht_out[...] = h2.T.astype(BF16)


def _postmix_call(x2, y_rwkv, pool_in, modl, pool_w, pool_scale, w_out, ln_g, tiles_per_batch):
    T, D = x2.shape
    row = lambda n: pl.BlockSpec((1, n), lambda i: (0, 0))
    full = lambda a: pl.BlockSpec(a.shape, lambda i: (0,) * a.ndim)
    tok = lambda n: pl.BlockSpec((TM, n), lambda i: (i, 0))
    return pl.pallas_call(
        functools.partial(_postmix_kernel, tiles_per_batch=tiles_per_batch),
        grid=(T // TM,),
        in_specs=[tok(D), tok(RWKV_WIDTH), tok(POOL_WIDTH),
                  pl.BlockSpec((1, 6, D), lambda i: (i // tiles_per_batch, 0, 0)),
                  full(pool_w), row(POOL_WIDTH), full(w_out), row(D)],
        out_specs=[tok(D), pl.BlockSpec((D, TM), lambda i: (0, i))],
        out_shape=[jax.ShapeDtypeStruct((T, D), F32), jax.ShapeDtypeStruct((D, T), BF16)],
        scratch_shapes=[pltpu.VMEM((MAX_WINDOW, POOL_WIDTH), F32)],
        compiler_params=pltpu.CompilerParams(dimension_semantics=("arbitrary",),
                                             vmem_limit_bytes=VMEM_LIMIT),
        name="postmix",
    )(x2, y_rwkv, pool_in, modl, pool_w, pool_scale, w_out, ln_g)


def _extract16(cur, mark, marks, tie_break):
    n, w = cur.shape
    iota = lax.broadcasted_iota(jnp.int32, (n, w), 0).astype(F32)
    vals = []
    for j in range(PEER_TOPK):
        mx = jnp.max(cur, axis=0, keepdims=True)
        hit = cur == mx
        if tie_break:
            first = jnp.min(jnp.where(hit, iota, float(n)), axis=0, keepdims=True)
            hit = iota == first
        mark = jnp.where(hit, marks[j], mark)
        cur = jnp.where(hit, -jnp.inf, cur)
        vals.append(mx)
    removed = jnp.sum((cur == -jnp.inf).astype(F32), axis=0, keepdims=True)
    return mark, vals, removed


def _pair_stage(vals1, v2_ref, cand_ref, bm_ref, tie_break):
    w = v2_ref.shape[1]
    for (a, nb), off in zip(_CAND_ROWS, _CAND_OFFS):
        cand_ref[off:off + nb, :] = vals1[a] + v2_ref[0:nb, :]
    cand_ref[N_CAND:N_CAND_PAD, :] = jnp.full((N_CAND_PAD - N_CAND, w), -jnp.inf, F32)
    cand = cand_ref[...]
    chosen, _, removed = _extract16(cand, jnp.zeros(cand.shape, F32), [1.0] * PEER_TOPK, tie_break)
    z = jnp.sum(chosen * jnp.exp(cand - cand[0:1, :]), axis=0, keepdims=True)
    cand_ref[...] = chosen
    for (a, nb), off in zip(_CAND_ROWS, _CAND_OFFS):
        bm_ref[a:a + 1, :] = jnp.sum(cand_ref[off:off + nb, :], axis=0, keepdims=True)
    return z, removed - float(N_CAND_PAD - N_CAND)


def _select_strip_exact(s1, s2, outs, lanes, v2_ref, cand_ref, bm_ref):
    rank2_out, e2_out, b_out, f_out = outs
    ranks = [float(j + 1) for j in range(PEER_TOPK)]
    unranked = jnp.full(s1.shape, 99.0, F32)
    rank1, vals1, _ = _extract16(s1, unranked, ranks, True)
    rank2, vals2, _ = _extract16(s2, unranked, ranks, True)
    for j in range(PEER_TOPK):
        v2_ref[j:j + 1, :] = vals2[j]
    z, _ = _pair_stage(vals1, v2_ref, cand_ref, bm_ref, True)
    bsel = jnp.zeros(s1.shape, F32)
    for a in range(PEER_TOPK):
        bsel = jnp.where(rank1 == ranks[a], bm_ref[a:a + 1, :], bsel)
    rank2_out[0, :, lanes] = rank2.astype(BF16)
    e2_out[0, :, lanes] = jnp.exp(s2 - vals2[0]).astype(BF16)
    b_out[0, :, lanes] = bsel
    f_out[0, :, lanes] = jnp.exp(s1 - vals1[0]) / z


def _oddeven_merge(lo, hi, r):
    step = r * 2
    if step < hi - lo:
        yield from _oddeven_merge(lo, hi, step)
        yield from _oddeven_merge(lo + r, hi, step)
        yield from [(i, i + r) for i in range(lo + r, hi - r, step)]
    else:
        yield (lo, lo + r)


def _oddeven_sort(lo, hi):
    if hi - lo >= 1:
        mid = lo + (hi - lo) // 2
        yield from _oddeven_sort(lo, mid)
        yield from _oddeven_sort(mid + 1, hi)
        yield from _oddeven_merge(lo, hi, 1)


_SORT16 = tuple(_oddeven_sort(0, PEER_TOPK - 1))


def _exchange(a, i, j):
    a[i], a[j] = jnp.maximum(a[i], a[j]), jnp.minimum(a[i], a[j])


def _sorted_top16(slabs):
    a = list(slabs)
    for i, j in _SORT16:
        _exchange(a, i, j)
    for shift in (4, 2, 1):
        other = [pltpu.roll(t, shift, 0) for t in a]
        a = [jnp.maximum(a[j], other[PEER_TOPK - 1 - j]) for j in range(PEER_TOPK)]
        d = PEER_TOPK // 2
        while d >= 1:
            for i in range(PEER_TOPK):
                if i & d == 0:
                    _exchange(a, i, i + d)
            d //= 2
    return a


def _count_above(x, v):
    c1 = v[7] > x
    c2 = jnp.where(c1, v[11], v[3]) > x
    c3 = jnp.where(c1, jnp.where(c2, v[13], v[9]), jnp.where(c2, v[5], v[1])) > x
    c4 = jnp.where(c1,
                   jnp.where(c2, jnp.where(c3, v[14], v[12]), jnp.where(c3, v[10], v[8])),
                   jnp.where(c2, jnp.where(c3, v[6], v[4]), jnp.where(c3, v[2], v[0]))) > x
    return c1, c2, c3, c4


def _pick16(masks, table):
    t = list(table)
    for c in reversed(masks):
        t = [jnp.where(c, t[2 * k + 1], t[2 * k]) for k in range(len(t) // 2)]
    return t[0]


def _select_strip_fast(s1, s2, outs, lanes, v2_ref, cand_ref, bm_ref):
    rank2_out, e2_out, b_out, f_out = outs
    w = s1.shape[1]
    n_slabs = s1.shape[0] // 8
    slabs1 = [s1[8 * k:8 * k + 8, :] for k in range(n_slabs)]
    slabs2 = [s2[8 * k:8 * k + 8, :] for k in range(n_slabs)]
    top1 = _sorted_top16(slabs1)
    top2 = _sorted_top16(slabs2)
    for j in range(PEER_TOPK):
        v2_ref[j:j + 1, :] = top2[j][0:1, :]
    z, removed = _pair_stage([t[0:1, :] for t in top1], v2_ref, cand_ref, bm_ref, False)
    counts = [jnp.broadcast_to(bm_ref[a:a + 1, :], (8, w)) for a in range(PEER_TOPK)]
    zb = jnp.broadcast_to(z, (8, w))
    suspect = jnp.zeros((8, w), F32)
    for j in range(PEER_TOPK - 1):
        suspect = jnp.where(top1[j] == top1[j + 1], 99.0, suspect)
        suspect = jnp.where(top2[j] == top2[j + 1], 99.0, suspect)
    in_top1 = jnp.zeros((8, w), F32)
    in_top2 = jnp.zeros((8, w), F32)
    rank2, e2, bsel, fsel = [], [], [], []
    for k in range(n_slabs):
        x1, x2 = slabs1[k], slabs2[k]
        keep1 = x1 >= top1[PEER_TOPK - 1]
        keep2 = x2 >= top2[PEER_TOPK - 1]
        in_top1 = in_top1 + jnp.where(keep1, 1.0, 0.0)
        in_top2 = in_top2 + jnp.where(keep2, 1.0, 0.0)
        bsel.append(jnp.where(keep1, _pick16(_count_above(x1, top1), counts), 0.0))
        c1, c2, c3, c4 = _count_above(x2, top2)
        rank = (1.0 + jnp.where(c1, 8.0, 0.0)) + (jnp.where(c2, 4.0, 0.0)
                                                  + (jnp.where(c3, 2.0, 0.0) + jnp.where(c4, 1.0, 0.0)))
        rank2.append(jnp.where(keep2, rank, 99.0))
        e2.append(jnp.exp(x2 - top2[0]))
        fsel.append(jnp.exp(x1 - top1[0]) / zb)
    rank2_out[0, :, lanes] = jnp.concatenate(rank2, axis=0).astype(BF16)
    e2_out[0, :, lanes] = jnp.concatenate(e2, axis=0).astype(BF16)
    b_out[0, :, lanes] = jnp.concatenate(bsel, axis=0)
    f_out[0, :, lanes] = jnp.concatenate(fsel, axis=0)
    kept = jnp.maximum(jnp.sum(in_top1, axis=0, keepdims=True), jnp.sum(in_top2, axis=0, keepdims=True))
    return jnp.maximum(jnp.max(jnp.maximum(kept, removed)), jnp.max(suspect))


def _keyproj_kernel(keys_ref, q_ref, o_ref):
    o_ref[...] = _dot(keys_ref[0], q_ref[...], _NT, precision=HIGHEST).astype(BF16)


def _keyproj_call(keys, peer_q):
    D = peer_q.shape[0]
    n = keys.shape[0] * keys.shape[1]
    return pl.pallas_call(
        _keyproj_kernel,
        grid=(n,),
        in_specs=[pl.BlockSpec((1, N_KEYS, N_KEYS), lambda i: (i, 0, 0)),
                  pl.BlockSpec((D, N_KEYS), lambda i: (0, i))],
        out_specs=pl.BlockSpec((N_KEYS, D), lambda i: (i, 0)),
        out_shape=jax.ShapeDtypeStruct((n * N_KEYS, D), BF16),
        name="keyproj",
    )(keys.reshape(n, N_KEYS, N_KEYS), peer_q)


def _select_kernel(ht_ref, ws_ref, rank2_out, e2_out, b_out, f_out,
                   s1_ref, s2_ref, v2_ref, cand_ref, bm_ref):
    s = _dot(ws_ref[...], ht_ref[...])
    s1_ref[...] = s[0:N_KEYS]
    s2_ref[...] = s[N_KEYS:2 * N_KEYS]
    outs = (rank2_out, e2_out, b_out, f_out)

    def strip(j, carry):
        lanes = pl.ds(pl.multiple_of(j * SELECT_STRIP, SELECT_STRIP), SELECT_STRIP)
        s1 = s1_ref[:, lanes]
        s2 = s2_ref[:, lanes]
        maybe_tied = _select_strip_fast(s1, s2, outs, lanes, v2_ref, cand_ref, bm_ref)

        @pl.when(maybe_tied > float(PEER_TOPK))
        def _():
            _select_strip_exact(s1, s2, outs, lanes, v2_ref, cand_ref, bm_ref)

        return carry

    lax.fori_loop(0, s1_ref.shape[1] // SELECT_STRIP, strip, 0)


def _select_call(h2t, ws):
    D, T = h2t.shape
    out = jax.ShapeDtypeStruct((PEER_HEADS, N_KEYS, T), F32)
    out16 = jax.ShapeDtypeStruct((PEER_HEADS, N_KEYS, T), BF16)
    ospec = pl.BlockSpec((1, N_KEYS, TQ), lambda i, h: (h, 0, i))
    return pl.pallas_call(
        _select_kernel,
        grid=(T // TQ, PEER_HEADS),
        in_specs=[pl.BlockSpec((D, TQ), lambda i, h: (0, i)),
                  pl.BlockSpec((2 * N_KEYS, D), lambda i, h: (h, 0))],
        out_specs=[ospec] * 4,
        out_shape=[out16, out16, out, out],
        scratch_shapes=[pltpu.VMEM((N_KEYS, TQ), F32), pltpu.VMEM((N_KEYS, TQ), F32),
                        pltpu.VMEM((PEER_TOPK, SELECT_STRIP), F32),
                        pltpu.VMEM((N_CAND_PAD, SELECT_STRIP), F32),
                        pltpu.VMEM((PEER_TOPK, SELECT_STRIP), F32)],
        compiler_params=pltpu.CompilerParams(dimension_semantics=("arbitrary", "arbitrary"),
                                             vmem_limit_bytes=VMEM_LIMIT),
        name="select",
    )(h2t, ws)


def _transpose_cast_kernel(x_ref, o_ref):
    o_ref[...] = x_ref[...].T.astype(BF16)


def _transpose_cast_call(w):
    n, d = w.shape
    return pl.pallas_call(
        _transpose_cast_kernel,
        grid=(n // EB,),
        in_specs=[pl.BlockSpec((EB, d), lambda i: (i, 0))],
        out_specs=pl.BlockSpec((d, EB), lambda i: (0, i)),
        out_shape=jax.ShapeDtypeStruct((d, n), BF16),
        name="transpose_cast",
    )(w)


def _experts_kernel(ht_ref, u_ref, vt_ref, rank2_ref, e2_ref, b_ref, f_ref, x_ref, mod_ref, lnf_ref,
                    x_out, acc_ref, *, tiles_per_batch, final):
    e = pl.program_id(1)

    @pl.when(e == 0)
    def _():
        acc_ref[...] = jnp.zeros_like(acc_ref)

    tt = ht_ref.shape[1]

    gates = []
    for ii in range(FIRST_KEYS_PER_STEP):
        gate = None
        for h in range(PEER_HEADS):
            brow = jnp.broadcast_to(b_ref[h, e, ii:ii + 1, :], (N_KEYS, tt)).astype(BF16)
            frow = jnp.broadcast_to(f_ref[h, e, ii:ii + 1, :], (N_KEYS, tt)).astype(BF16)
            term = jnp.where(rank2_ref[h] <= brow, e2_ref[h], 0.0) * frow
            gate = term if gate is None else gate + term
        gates.append(gate)
    act = _dot(u_ref[...], ht_ref[...]).astype(BF16)
    gelu = 0.5 * act * (1.0 + lax.erf(act * (2.0 ** -0.5)))
    z = jnp.concatenate(gates, axis=0) * gelu
    acc_ref[...] += _dot(vt_ref[...], z)

    @pl.when(e == pl.num_programs(1) - 1)
    def _():
        m = mod_ref[0]
        x = x_ref[...] + m[5:6] * acc_ref[...].T
        if final:
            x = x * lax.rsqrt(jnp.mean(x * x, axis=-1, keepdims=True) + NORM_EPS) * lnf_ref[...]
        x_out[...] = x


def _experts_call(h2t, u_bf, vt_bf, rank2, e2, bsel, fsel, x2, modl, lnf_g, tiles_per_batch, final):
    D, T = h2t.shape
    NE = u_bf.shape[0]
    once = dict(pipeline_mode=pl.Buffered(1))
    sel = pl.BlockSpec((PEER_HEADS, N_KEYS, TT), lambda i, e: (0, 0, i), **once)
    bsel, fsel = (t.reshape(PEER_HEADS, N_KEYS // FIRST_KEYS_PER_STEP, FIRST_KEYS_PER_STEP, T)
                  for t in (bsel, fsel))
    sel1 = pl.BlockSpec(bsel.shape[:3] + (TT,), lambda i, e: (0, 0, 0, i), **once)
    tok = pl.BlockSpec((TT, D), lambda i, e: (i, 0))
    return pl.pallas_call(
        functools.partial(_experts_kernel, tiles_per_batch=tiles_per_batch, final=final),
        grid=(T // TT, NE // EB),
        in_specs=[pl.BlockSpec((D, TT), lambda i, e: (0, i), **once),
                  pl.BlockSpec((EB, D), lambda i, e: (e, 0)),
                  pl.BlockSpec((D, EB), lambda i, e: (0, e)),
                  sel, sel, sel1, sel1, pl.BlockSpec((TT, D), lambda i, e: (i, 0), **once),
                  pl.BlockSpec((1, 6, D), lambda i, e: (i // tiles_per_batch, 0, 0)),
                  pl.BlockSpec((1, D), lambda i, e: (0, 0))],
        out_specs=tok,
        out_shape=jax.ShapeDtypeStruct((T, D), F32),
        scratch_shapes=[pltpu.VMEM((D, TT), F32)],
        compiler_params=pltpu.CompilerParams(dimension_semantics=("arbitrary", "arbitrary"),
                                             vmem_limit_bytes=VMEM_LIMIT),
        name="experts",
    )(h2t, u_bf, vt_bf, rank2, e2, bsel, fsel, x2, modl, lnf_g)


def kernel(x, c, ada_w, ada_b, ln1_g, w_in, mu_shift, w0, w_up, a0, a_up, g_up, vres_down, vres_mu,
           vres_v0, vres_up, k_k, k_a, r_k, lnx_g, lnx_b, pool_w, pool_scale, w_out, ln2_g, peer_q,
           peer_keys, peer_u, peer_v, lnf_g):
    B, S, D = x.shape
    L = ada_w.shape[0]
    T = B * S
    assert S % TS == 0 and S % TM == 0 and T % TQ == 0 and T % TT == 0 and TT <= S
    assert B % RWKV_SEQS_PER_STEP == 0
    row = lambda t: t.reshape(1, -1)

    c_pad = jnp.zeros((8, D), F32).at[:B].set(c)
    mod = _ada_call(c_pad, ada_w, ada_b)[:, :B].reshape(L, B, 6, D)

    half = LORA_COLS // 2
    x2 = x.reshape(T, D)
    v_first = None
    for l in range(L):
        if l == 0:
            w_full = w_in[l].astype(BF16)
            vres = None
        else:
            mv = vres_down.shape[-1]
            w_full = jnp.concatenate(
                [w_in[l], vres_down[l - 1], jnp.zeros((D, VRES_PAD - mv), F32)], axis=1).astype(BF16)
            vres = (jnp.zeros((1, VRES_PAD), F32).at[:, :mv].set(vres_mu[l - 1]),
                    row(vres_v0[l - 1]),
                    jnp.zeros((VRES_PAD, RWKV_WIDTH), F32).at[:mv].set(vres_up[l - 1]),
                    v_first)
        lora_w = jnp.zeros((LORA_COLS, 2 * RWKV_WIDTH), F32)
        lora_w = lora_w.at[:half, :RWKV_WIDTH].set(w_up[l]).at[half:, RWKV_WIDTH:].set(a_up[l])
        r, k2, v, lw, kk, a, g, pool_in = _premix_call(
            x2, mod[l], row(ln1_g[l]), w_full, row(mu_shift[l]), row(w0[l]), row(a0[l]), lora_w,
            g_up[l], row(k_k[l]), row(k_a[l]), vres, S // TM)
        if l == 0:
            v_first = v
        y_rwkv = _rwkv_call(r, k2, v, lw, kk, a, g, row(r_k[l]), row(lnx_g[l]), row(lnx_b[l]), B)
        x_mid, h2t = _postmix_call(x2, y_rwkv, pool_in, mod[l], pool_w[l], row(pool_scale[l]),
                                   w_out[l].astype(BF16), row(ln2_g[l]), S // TM)
        ws = _keyproj_call(peer_keys[l], peer_q[l])
        rank2, e2, bsel, fsel = _select_call(h2t, ws)
        x2 = _experts_call(h2t, peer_u[l].astype(BF16), _transpose_cast_call(peer_v[l]), rank2, e2, bsel, fsel,
                           x_mid, mod[l], row(lnf_g), S // TT, l == L - 1)
    return x2.reshape(B, S, D)
```

```python
import functools

import jax
import jax.numpy as jnp
from jax import lax
from jax.experimental import pallas as pl
from jax.experimental.pallas import tpu as pltpu

F32 = jnp.float32
BF16 = jnp.bfloat16
HIGHEST = lax.Precision.HIGHEST

HEAD_SIZE = 64
RWKV_WIDTH = 512
POOL_WIDTH = 512
POOL_GROUP = 128
POOL_WINDOWS = (2, 4, 8, 16)
MAX_WINDOW = 16
RWKV_COLS = 1792
LORA_COLS = 128
GATE_LORA = 128
VRES_PAD = 128
GN_EPS = 64e-5
NORM_EPS = 1e-6
N_KEYS = 128
PEER_HEADS = 8
PEER_TOPK = 16
CHUNK = 64
PAIR = 128
RWKV_SEQS_PER_STEP = 2

TM = 512
TS = 512
TQ = 1024
SELECT_HEADS = 2
SELECT_STRIP = 128
TT = 1024
EB = 1024
FIRST_KEYS_PER_STEP = EB // N_KEYS

VMEM_LIMIT = 56 * 1024 * 1024

_CAND_ROWS = [(a, PEER_TOPK // (a + 1)) for a in range(PEER_TOPK)]
_CAND_OFFS = []
_off = 0
for _a, _nb in _CAND_ROWS:
    _CAND_OFFS.append(_off)
    _off += _nb
N_CAND = _off
N_CAND_PAD = -(-N_CAND // 8) * 8


def _dot(a, b, dims=(((1,), (0,)), ((), ())), precision=None):
    return lax.dot_general(a, b, dims, precision=precision, preferred_element_type=F32)


_NT = (((1,), (1,)), ((), ()))
_TN = (((0,), (0,)), ((), ()))
_NN = (((1,), (0,)), ((), ()))


def _dot_split(a, b, pieces, split_lhs=True):
    rem = a if split_lhs else b
    acc = None
    for i in range(pieces):
        hi = rem.astype(BF16)
        part = _dot(hi, b, _NN) if split_lhs else _dot(a, hi, _NN)
        acc = part if acc is None else acc + part
        if i + 1 < pieces:
            rem = rem - hi.astype(F32)
    return acc


def _dot3(a, b):
    ah = a.astype(BF16)
    bh = b.astype(BF16)
    al = (a - ah.astype(F32)).astype(BF16)
    bl = (b - bh.astype(F32)).astype(BF16)
    return _dot(ah, bh) + (_dot(ah, bl) + _dot(al, bh))


def _group_ones(n, group):
    r = lax.broadcasted_iota(jnp.int32, (n, n), 0) // group
    c = lax.broadcasted_iota(jnp.int32, (n, n), 1) // group
    return (r == c).astype(F32)


def _rms_mod(x, g, shift, scale):
    y = x * lax.rsqrt(jnp.mean(x * x, axis=-1, keepdims=True) + NORM_EPS)
    return (y * g) * (1.0 + scale) + shift


def _ada_kernel(c_ref, w_ref, b_ref, o_ref):
    c = c_ref[...]
    ca = c * jax.nn.sigmoid(c)
    o_ref[0] = _dot(ca, w_ref[0], precision=HIGHEST) + b_ref[0]


def _ada_call(c_pad, ada_w, ada_b):
    L, D, D6 = ada_w.shape
    nb = D6 // D
    return pl.pallas_call(
        _ada_kernel,
        grid=(L, nb),
        in_specs=[
            pl.BlockSpec((8, D), lambda l, j: (0, 0)),
            pl.BlockSpec((1, D, D), lambda l, j: (l, 0, j)),
            pl.BlockSpec((1, 1, D), lambda l, j: (l, 0, j)),
        ],
        out_specs=pl.BlockSpec((1, 8, D), lambda l, j: (l, 0, j)),
        out_shape=jax.ShapeDtypeStruct((L, 8, D6), F32),
        name="adaln",
    )(c_pad, ada_w, ada_b.reshape(L, 1, D6))


def _premix_kernel(*refs, tiles_per_batch, has_vres):
    if has_vres:
        (x_ref, mod_ref, ln_ref, w_ref, mu_ref, w0_ref, a0_ref, lora_ref, gup_ref, kk_ref, ka_ref,
         vmu_ref, v0_ref, vup_ref, vfirst_ref,
         r_out, k_out, v_out, lw_out, kkn_out, a_out, g_out, p_out, carry_ref) = refs
    else:
        (x_ref, mod_ref, ln_ref, w_ref, mu_ref, w0_ref, a0_ref, lora_ref, gup_ref, kk_ref, ka_ref,
         r_out, k_out, v_out, lw_out, kkn_out, a_out, g_out, p_out, carry_ref) = refs
    i = pl.program_id(0)

    @pl.when(i % tiles_per_batch == 0)
    def _():
        carry_ref[...] = jnp.zeros_like(carry_ref)

    m = mod_ref[0]
    h = _rms_mod(x_ref[...], ln_ref[...], m[0:1], m[1:2])
    proj = _dot(h.astype(BF16), w_ref[...])
    tm = proj.shape[0]
    row0 = lax.broadcasted_iota(jnp.int32, (tm, 1), 0) == 0
    carry = carry_ref[...]
    carry_ref[...] = proj[tm - 1:tm, :]

    def shifted(lo, hi, mu):
        z = proj[:, lo:hi]
        prev = jnp.where(row0, carry[:, lo:hi], pltpu.roll(z, 1, 0))
        return z + (prev - z) * mu

    rw = shifted(0, RWKV_COLS, mu_ref[...])
    r = rw[:, 0:RWKV_WIDTH]
    k = rw[:, RWKV_WIDTH:2 * RWKV_WIDTH]
    v = rw[:, 2 * RWKV_WIDTH:3 * RWKV_WIDTH]
    lo = 3 * RWKV_WIDTH
    wa = rw[:, lo:lo + LORA_COLS]
    gd = rw[:, lo + LORA_COLS:lo + LORA_COLS + GATE_LORA]
    lane = lax.broadcasted_iota(jnp.int32, wa.shape, 1)
    wa = jnp.where(lane < LORA_COLS // 2, jnp.tanh(wa), wa)
    lora = _dot3(wa, lora_ref[...])
    zw = -(w0_ref[...] + lora[:, 0:RWKV_WIDTH])
    softplus = jnp.maximum(zw, 0.0) + jnp.log1p(jnp.exp(-jnp.abs(zw)))
    w = -softplus - 0.5
    a = jax.nn.sigmoid(a0_ref[...] + lora[:, RWKV_WIDTH:2 * RWKV_WIDTH])
    g = _dot3(jax.nn.sigmoid(gd), gup_ref[...])
    kk = k * kk_ref[...]
    ss = _dot_split(kk * kk, _group_ones(RWKV_WIDTH, HEAD_SIZE).astype(BF16), 3)
    kk = kk / jnp.maximum(jnp.sqrt(ss), 1e-12)
    k2 = k * (1.0 + (a - 1.0) * ka_ref[...])
    if has_vres:
        base = RWKV_COLS + POOL_WIDTH
        vd = shifted(base, base + VRES_PAD, vmu_ref[...])
        mixv = jax.nn.sigmoid(v0_ref[...] + _dot3(vd, vup_ref[...]))
        v = v + (vfirst_ref[...] - v) * mixv
    r_out[...] = r
    k_out[...] = k2
    v_out[...] = v
    lw_out[...] = -jnp.exp(w)
    kkn_out[...] = kk
    a_out[...] = a
    g_out[...] = g
    p_out[...] = proj[:, RWKV_COLS:RWKV_COLS + POOL_WIDTH]


def _premix_call(x2, modl, ln_g, w_full, mu, w0, a0, lora_w, g_up, k_k, k_a, vres, tiles_per_batch):
    T, D = x2.shape
    NC = w_full.shape[1]
    has_vres = vres is not None
    row = lambda n: pl.BlockSpec((1, n), lambda i: (0, 0))
    full = lambda a: pl.BlockSpec(a.shape, lambda i: (0,) * a.ndim)
    tok = lambda n: pl.BlockSpec((TM, n), lambda i: (i, 0))
    in_specs = [tok(D), pl.BlockSpec((1, 6, D), lambda i: (i // tiles_per_batch, 0, 0)), row(D),
                full(w_full), row(RWKV_COLS), row(RWKV_WIDTH), row(RWKV_WIDTH), full(lora_w), full(g_up),
                row(RWKV_WIDTH), row(RWKV_WIDTH)]
    args = [x2, modl, ln_g, w_full, mu, w0, a0, lora_w, g_up, k_k, k_a]
    if has_vres:
        vmu, v0, vup, vfirst = vres
        in_specs += [row(VRES_PAD), row(RWKV_WIDTH), full(vup), tok(RWKV_WIDTH)]
        args += [vmu, v0, vup, vfirst]
    out = jax.ShapeDtypeStruct((T, RWKV_WIDTH), F32)
    return pl.pallas_call(
        functools.partial(_premix_kernel, tiles_per_batch=tiles_per_batch, has_vres=has_vres),
        grid=(T // TM,),
        in_specs=in_specs,
        out_specs=[tok(RWKV_WIDTH)] * 8,
        out_shape=[out] * 8,
        scratch_shapes=[pltpu.VMEM((1, NC), F32)],
        compiler_params=pltpu.CompilerParams(dimension_semantics=("arbitrary",),
                                             vmem_limit_bytes=VMEM_LIMIT),
        name="premix",
    )(*args)


def _rwkv_kernel(r_ref, k_ref, v_ref, lw_ref, kk_ref, a_ref, g_ref, rk_ref, lng_ref, lnb_ref,
                 o_ref, state_ref, y_ref, ra_ref, yu_ref, xbk_ref, xv_ref, gc_ref):
    @pl.when(pl.program_id(1) == 0)
    def _():
        state_ref[...] = jnp.zeros_like(state_ref)

    n_seq, ts, width = r_ref.shape
    n_pairs = width // PAIR
    n_chunks = ts // CHUNK
    ti = lax.broadcasted_iota(jnp.int32, (CHUNK, CHUNK), 0)
    si = lax.broadcasted_iota(jnp.int32, (CHUNK, CHUNK), 1)
    tril = (ti >= si).astype(BF16)
    ri = lax.broadcasted_iota(jnp.int32, (2 * PAIR, 2 * PAIR), 0)
    ci = lax.broadcasted_iota(jnp.int32, (2 * PAIR, 2 * PAIR), 1)
    keep = (ri % CHUNK) + (ri >= PAIR).astype(jnp.int32) > (ci % CHUNK)
    ei = lax.broadcasted_iota(jnp.int32, (PAIR, PAIR), 0)
    ej = lax.broadcasted_iota(jnp.int32, (PAIR, PAIR), 1)
    eye = (ei == ej).astype(F32)
    lane = lax.broadcasted_iota(jnp.int32, (CHUNK, PAIR), 1)
    head0 = lane < HEAD_SIZE
    zeros_pp = jnp.zeros((PAIR, PAIR), BF16)

    def stack(t):
        return jnp.concatenate([jnp.where(head0, t, 0.0), jnp.where(head0, 0.0, t)], axis=0).astype(BF16)

    def pre_body(b, carry):
        chains = []
        for cc in range(n_chunks):
            c = b * n_chunks + cc
            rows = slice(cc * CHUNK, (cc + 1) * CHUNK)
            lw = lw_ref[b, rows, :]
            cum = _dot_split(tril, lw, 3, split_lhs=False)
            g_in = jnp.exp(cum)
            g_ex = jnp.exp(cum - lw)
            g_inv = jnp.exp(-cum)
            gc_ref[c] = g_in[CHUNK - 1:CHUNK, :]
            kk = kk_ref[b, rows, :]
            rt = r_ref[b, rows, :] * g_in
            kt = k_ref[b, rows, :] * g_inv
            bt = (kk * a_ref[b, rows, :]) * g_inv
            at = -kk * g_ex
            vv = v_ref[b, rows, :]
            for p in range(n_pairs):
                ls = slice(p * PAIR, (p + 1) * PAIR)
                xa, xb, xk, xr, xv = (stack(t[:, ls]) for t in (at, bt, kt, rt, vv))
                xbk = jnp.concatenate([xb, xk], axis=0)
                xbk_ref[c, p] = xbk
                xv_ref[c, p] = xv
                chains.append(dict(c=c, p=p, xa=xa, xr=xr, xv=xv, xbk=xbk))
        for ch in chains:
            big = _dot(jnp.concatenate([ch["xa"], ch["xr"]], axis=0), ch["xbk"], _NT)
            big = jnp.where(keep, big, 0.0)
            ch["lab"] = big[0:PAIR, 0:PAIR]
            ch["lak"] = big[0:PAIR, PAIR:2 * PAIR].astype(BF16)
            ch["lrbk"] = big[PAIR:2 * PAIR, :].astype(BF16)
        for ch in chains:
            lb = ch["lab"].astype(BF16)
            ch["pw"] = _dot(lb, lb)
            ch["lakv"] = _dot(ch["lak"], ch["xv"]).astype(BF16)
            ch["inv"] = eye + ch["lab"]
        for i in range(1, 6):
            for ch in chains:
                pb = ch["pw"].astype(BF16)
                sb = ch["inv"].astype(BF16)
                if i < 5:
                    res = _dot(pb, jnp.concatenate([pb, sb], axis=1))
                    ch["pw"] = res[:, 0:PAIR]
                    ch["inv"] = ch["inv"] + res[:, PAIR:2 * PAIR]
                else:
                    ch["inv"] = ch["inv"] + _dot(pb, sb)
        for ch in chains:
            au = _dot(ch["inv"].astype(BF16), jnp.concatenate([ch["xa"], ch["lakv"]], axis=1))
            ch["a2"] = au[:, 0:PAIR].astype(BF16)
            ch["uv"] = au[:, PAIR:2 * PAIR]
        for ch in chains:
            rhs = jnp.concatenate(
                [jnp.concatenate([ch["a2"], ch["uv"].astype(BF16)], axis=1),
                 jnp.concatenate([zeros_pp, ch["xv"]], axis=1)], axis=0)
            ry = _dot(ch["lrbk"], rhs)
            r2 = ch["xr"].astype(F32) + ry[:, 0:PAIR]
            ra_ref[ch["c"], ch["p"]] = jnp.concatenate([r2.astype(BF16), ch["a2"]], axis=0)
            yu_ref[ch["c"], ch["p"]] = jnp.concatenate([ry[:, PAIR:2 * PAIR], ch["uv"]], axis=0)
        return carry

    lax.fori_loop(0, n_seq, pre_body, 0)

    def seq_body(cc, carry):
        rows = pl.ds(pl.multiple_of(cc * CHUNK, CHUNK), CHUNK)
        chains = [(b, p, b * n_chunks + cc) for b in range(n_seq) for p in range(n_pairs)]
        res = [_dot(ra_ref[c, p], state_ref[b, p].astype(BF16), _NT) + yu_ref[c, p] for b, p, c in chains]
        for (b, p, c), res_bp in zip(chains, res):
            ls = slice(p * PAIR, (p + 1) * PAIR)
            y_st = res_bp[0:PAIR]
            u_st = res_bp[PAIR:2 * PAIR].astype(BF16)
            st = _dot(jnp.concatenate([u_st, xv_ref[c, p]], axis=0), xbk_ref[c, p], _TN)
            state_ref[b, p] = (state_ref[b, p] + st) * gc_ref[c][:, ls]
            y_ref[b, rows, ls] = y_st[0:CHUNK] + y_st[CHUNK:2 * CHUNK]
        return carry

    lax.fori_loop(0, n_chunks, seq_body, 0)

    ones = _group_ones(RWKV_WIDTH, HEAD_SIZE).astype(BF16)
    flat = lambda ref: ref[...].reshape(n_seq * ts, width)
    y = flat(y_ref)
    mean = _dot_split(y, ones, 2) * (1.0 / HEAD_SIZE)
    d = y - mean
    var = _dot_split(d * d, ones, 2) * (1.0 / HEAD_SIZE)
    yn = d * lax.rsqrt(var + GN_EPS) * lng_ref[...] + lnb_ref[...]
    bonus = _dot_split(flat(r_ref) * flat(k_ref) * rk_ref[...], ones, 2) * flat(v_ref)
    o_ref[...] = ((yn + bonus) * flat(g_ref)).reshape(n_seq, ts, width)


def _rwkv_call(r, k, v, lw, kk, a, g, r_k, lnx_g, lnx_b, batch):
    T, W = r.shape
    seq = T // batch
    ts = TS // RWKV_SEQS_PER_STEP
    n_chunks, n_pairs = TS // CHUNK, W // PAIR
    tok = pl.BlockSpec((RWKV_SEQS_PER_STEP, ts, W), lambda b, j: (b, j, 0))
    row = pl.BlockSpec((1, W), lambda b, j: (0, 0))
    per_seq = lambda t: t.reshape(batch, seq, W)
    return pl.pallas_call(
        _rwkv_kernel,
        grid=(batch // RWKV_SEQS_PER_STEP, seq // ts),
        in_specs=[tok] * 7 + [row] * 3,
        out_specs=tok,
        out_shape=jax.ShapeDtypeStruct((batch, seq, W), F32),
        scratch_shapes=[pltpu.VMEM((RWKV_SEQS_PER_STEP, n_pairs, PAIR, PAIR), F32),
                        pltpu.VMEM((RWKV_SEQS_PER_STEP, ts, W), F32),
                        pltpu.VMEM((n_chunks, n_pairs, 2 * PAIR, PAIR), BF16),
                        pltpu.VMEM((n_chunks, n_pairs, 2 * PAIR, PAIR), F32),
                        pltpu.VMEM((n_chunks, n_pairs, 2 * PAIR, PAIR), BF16),
                        pltpu.VMEM((n_chunks, n_pairs, PAIR, PAIR), BF16),
                        pltpu.VMEM((n_chunks, 1, W), F32)],
        compiler_params=pltpu.CompilerParams(dimension_semantics=("arbitrary", "arbitrary"),
                                             vmem_limit_bytes=VMEM_LIMIT),
        name="rwkv",
    )(*(per_seq(t) for t in (r, k, v, lw, kk, a, g)), r_k, lnx_g, lnx_b).reshape(T, W)


def _postmix_kernel(x_ref, y_ref, p_ref, mod_ref, pw_ref, ps_ref, wo_ref, ln_ref,
                    x_out, ht_out, carry_ref, *, tiles_per_batch):
    i = pl.program_id(0)
    j = i % tiles_per_batch

    @pl.when(j == 0)
    def _():
        carry_ref[...] = jnp.zeros_like(carry_ref)

    m = mod_ref[0]
    p = p_ref[...]
    tm = p.shape[0]
    ext = jnp.concatenate([carry_ref[...], p], axis=0)
    carry_ref[...] = p[tm - MAX_WINDOW:tm, :]
    pos = (j * tm + 1 + lax.broadcasted_iota(jnp.int32, (tm, 1), 0)).astype(F32)
    pooled = []
    for gi, win in enumerate(POOL_WINDOWS):
        ls = slice(gi * POOL_GROUP, (gi + 1) * POOL_GROUP)
        wsum = ext[:, ls]
        span = 1
        while span < win:
            wsum = wsum + pltpu.roll(wsum, span, 0)
            span *= 2
        mean = wsum[MAX_WINDOW:, :] / jnp.minimum(pos, float(win))
        d = mean - p[:, ls]
        pooled.append(_dot3(d, pw_ref[gi]))
    y_pool = jnp.concatenate(pooled, axis=1) * ps_ref[...]
    mix = (_dot(y_ref[...].astype(BF16), wo_ref[0:RWKV_WIDTH, :])
           + _dot(y_pool.astype(BF16), wo_ref[RWKV_WIDTH:RWKV_WIDTH + POOL_WIDTH, :]))
    x = x_ref[...] + m[2:3] * mix
    x_out[...] = x
    h2 = _rms_mod(x, ln_ref[...], m[3:4], m[4:5])
    ht_out[...] = h2.T.astype(BF16)


def _postmix_call(x2, y_rwkv, pool_in, modl, pool_w, pool_scale, w_out, ln_g, tiles_per_batch):
    T, D = x2.shape
    row = lambda n: pl.BlockSpec((1, n), lambda i: (0, 0))
    full = lambda a: pl.BlockSpec(a.shape, lambda i: (0,) * a.ndim)
    tok = lambda n: pl.BlockSpec((TM, n), lambda i: (i, 0))
    return pl.pallas_call(
        functools.partial(_postmix_kernel, tiles_per_batch=tiles_per_batch),
        grid=(T // TM,),
        in_specs=[tok(D), tok(RWKV_WIDTH), tok(POOL_WIDTH),
                  pl.BlockSpec((1, 6, D), lambda i: (i // tiles_per_batch, 0, 0)),
                  full(pool_w), row(POOL_WIDTH), full(w_out), row(D)],
        out_specs=[tok(D), pl.BlockSpec((D, TM), lambda i: (0, i))],
        out_shape=[jax.ShapeDtypeStruct((T, D), F32), jax.ShapeDtypeStruct((D, T), BF16)],
        scratch_shapes=[pltpu.VMEM((MAX_WINDOW, POOL_WIDTH), F32)],
        compiler_params=pltpu.CompilerParams(dimension_semantics=("arbitrary",),
                                             vmem_limit_bytes=VMEM_LIMIT),
        name="postmix",
    )(x2, y_rwkv, pool_in, modl, pool_w, pool_scale, w_out, ln_g)


def _extract16(cur, mark, marks, tie_break):
    n, w = cur.shape
    iota = lax.broadcasted_iota(jnp.int32, (n, w), 0).astype(F32)
    vals = []
    for j in range(PEER_TOPK):
        mx = jnp.max(cur, axis=0, keepdims=True)
        hit = cur == mx
        if tie_break:
            first = jnp.min(jnp.where(hit, iota, float(n)), axis=0, keepdims=True)
            hit = iota == first
        mark = jnp.where(hit, marks[j], mark)
        cur = jnp.where(hit, -jnp.inf, cur)
        vals.append(mx)
    removed = jnp.sum((cur == -jnp.inf).astype(F32), axis=0, keepdims=True)
    return mark, vals, removed


def _pair_stage(vals1, v2_ref, cand_ref, bm_ref, tie_break):
    w = v2_ref.shape[1]
    for (a, nb), off in zip(_CAND_ROWS, _CAND_OFFS):
        cand_ref[off:off + nb, :] = vals1[a] + v2_ref[0:nb, :]
    cand_ref[N_CAND:N_CAND_PAD, :] = jnp.full((N_CAND_PAD - N_CAND, w), -jnp.inf, F32)
    cand = cand_ref[...]
    chosen, _, removed = _extract16(cand, jnp.zeros(cand.shape, F32), [1.0] * PEER_TOPK, tie_break)
    z = jnp.sum(chosen * jnp.exp(cand - cand[0:1, :]), axis=0, keepdims=True)
    cand_ref[...] = chosen
    for (a, nb), off in zip(_CAND_ROWS, _CAND_OFFS):
        bm_ref[a:a + 1, :] = jnp.sum(cand_ref[off:off + nb, :], axis=0, keepdims=True)
    return z, removed - float(N_CAND_PAD - N_CAND)


def _select_strip_exact(s1, s2, outs, head, lanes, v2_ref, cand_ref, bm_ref):
    rank2_out, e2_out, b_out, f_out = outs
    ranks = [float(j + 1) for j in range(PEER_TOPK)]
    unranked = jnp.full(s1.shape, 99.0, F32)
    rank1, vals1, _ = _extract16(s1, unranked, ranks, True)
    rank2, vals2, _ = _extract16(s2, unranked, ranks, True)
    for j in range(PEER_TOPK):
        v2_ref[j:j + 1, :] = vals2[j]
    z, _ = _pair_stage(vals1, v2_ref, cand_ref, bm_ref, True)
    bsel = jnp.zeros(s1.shape, F32)
    for a in range(PEER_TOPK):
        bsel = jnp.where(rank1 == ranks[a], bm_ref[a:a + 1, :], bsel)
    rank2_out[head, :, lanes] = rank2.astype(BF16)
    e2_out[head, :, lanes] = jnp.exp(s2 - vals2[0]).astype(BF16)
    b_out[head, :, lanes] = bsel
    f_out[head, :, lanes] = jnp.exp(s1 - vals1[0]) / z


def _oddeven_merge(lo, hi, r):
    step = r * 2
    if step < hi - lo:
        yield from _oddeven_merge(lo, hi, step)
        yield from _oddeven_merge(lo + r, hi, step)
        yield from [(i, i + r) for i in range(lo + r, hi - r, step)]
    else:
        yield (lo, lo + r)


def _oddeven_sort(lo, hi):
    if hi - lo >= 1:
        mid = lo + (hi - lo) // 2
        yield from _oddeven_sort(lo, mid)
        yield from _oddeven_sort(mid + 1, hi)
        yield from _oddeven_merge(lo, hi, 1)


def _exchange(a, i, j):
    a[i], a[j] = jnp.maximum(a[i], a[j]), jnp.minimum(a[i], a[j])


def _sorted_top16(slabs):
    a = list(slabs)
    for i, j in _oddeven_sort(0, len(a) - 1):
        _exchange(a, i, j)
    for shift in (4, 2, 1):
        other = [pltpu.roll(t, shift, 0) for t in a]
        if len(a) < PEER_TOPK:
            a = a + other[::-1]
        else:
            a = [jnp.maximum(a[j], other[PEER_TOPK - 1 - j]) for j in range(PEER_TOPK)]
        d = PEER_TOPK // 2
        while d >= 1:
            for i in range(PEER_TOPK):
                if i & d == 0:
                    _exchange(a, i, i + d)
            d //= 2
    return a


def _count_above(x, v):
    c1 = v[7] > x
    c2 = jnp.where(c1, v[11], v[3]) > x
    c3 = jnp.where(c1, jnp.where(c2, v[13], v[9]), jnp.where(c2, v[5], v[1])) > x
    c4 = jnp.where(c1,
                   jnp.where(c2, jnp.where(c3, v[14], v[12]), jnp.where(c3, v[10], v[8])),
                   jnp.where(c2, jnp.where(c3, v[6], v[4]), jnp.where(c3, v[2], v[0]))) > x
    return c1, c2, c3, c4


def _pick16(masks, table):
    t = list(table)
    for c in reversed(masks):
        t = [jnp.where(c, t[2 * k + 1], t[2 * k]) for k in range(len(t) // 2)]
    return t[0]


def _select_strip_fast(s1, s2, outs, head, lanes, v2_ref, cand_ref, bm_ref):
    rank2_out, e2_out, b_out, f_out = outs
    w = s1.shape[1]
    n_slabs = s1.shape[0] // 8
    slabs1 = [s1[8 * k:8 * k + 8, :] for k in range(n_slabs)]
    slabs2 = [s2[8 * k:8 * k + 8, :] for k in range(n_slabs)]
    top1 = _sorted_top16(slabs1)
    top2 = _sorted_top16(slabs2)
    for j in range(PEER_TOPK):
        v2_ref[j:j + 1, :] = top2[j][0:1, :]
    for (a, nb), off in zip(_CAND_ROWS, _CAND_OFFS):
        cand_ref[off:off + nb, :] = top1[a][0:1, :] + v2_ref[0:nb, :]
    cand_ref[N_CAND:N_CAND_PAD, :] = jnp.full((N_CAND_PAD - N_CAND, w), -jnp.inf, F32)
    cand = cand_ref[...]
    cand_slabs = [cand[8 * k:8 * k + 8, :] for k in range(N_CAND_PAD // 8)]
    cand_slabs.append(jnp.full((8, w), -jnp.inf, F32))
    tau = _sorted_top16(cand_slabs)[PEER_TOPK - 1]
    chosen = cand >= tau[0:1, :]
    z = jnp.sum(jnp.where(chosen, jnp.exp(cand - cand[0:1, :]), 0.0), axis=0, keepdims=True)
    removed = jnp.sum(jnp.where(chosen, 1.0, 0.0), axis=0, keepdims=True)
    counts = []
    for a, nb in _CAND_ROWS:
        hits = [jnp.where(top1[a] + top2[b] >= tau, 1.0, 0.0) for b in range(nb)]
        counts.append(functools.reduce(lambda p, q: p + q, hits))
    zb = jnp.broadcast_to(z, (8, w))
    suspect = jnp.zeros((8, w), F32)
    for j in range(PEER_TOPK - 1):
        suspect = jnp.where(top1[j] == top1[j + 1], 99.0, suspect)
        suspect = jnp.where(top2[j] == top2[j + 1], 99.0, suspect)
    in_top1 = jnp.zeros((8, w), F32)
    in_top2 = jnp.zeros((8, w), F32)
    rank2, e2, bsel, fsel = [], [], [], []
    for k in range(n_slabs):
        x1, x2 = slabs1[k], slabs2[k]
        keep1 = x1 >= top1[PEER_TOPK - 1]
        keep2 = x2 >= top2[PEER_TOPK - 1]
        in_top1 = in_top1 + jnp.where(keep1, 1.0, 0.0)
        in_top2 = in_top2 + jnp.where(keep2, 1.0, 0.0)
        bsel.append(jnp.where(keep1, _pick16(_count_above(x1, top1), counts), 0.0))
        c1, c2, c3, c4 = _count_above(x2, top2)
        rank = (1.0 + jnp.where(c1, 8.0, 0.0)) + (jnp.where(c2, 4.0, 0.0)
                                                  + (jnp.where(c3, 2.0, 0.0) + jnp.where(c4, 1.0, 0.0)))
        rank2.append(jnp.where(keep2, rank, 99.0))
        e2.append(jnp.exp(x2 - top2[0]))
        fsel.append(jnp.exp(x1 - top1[0]) / zb)
    rank2_out[head, :, lanes] = jnp.concatenate(rank2, axis=0).astype(BF16)
    e2_out[head, :, lanes] = jnp.concatenate(e2, axis=0).astype(BF16)
    b_out[head, :, lanes] = jnp.concatenate(bsel, axis=0)
    f_out[head, :, lanes] = jnp.concatenate(fsel, axis=0)
    kept = jnp.maximum(jnp.sum(in_top1, axis=0, keepdims=True), jnp.sum(in_top2, axis=0, keepdims=True))
    return jnp.maximum(jnp.max(jnp.maximum(kept, removed)), jnp.max(suspect))


def _keyproj_kernel(keys_ref, q_ref, o_ref):
    o_ref[...] = _dot(keys_ref[0], q_ref[...], _NT, precision=HIGHEST).astype(BF16)


def _keyproj_call(keys, peer_q):
    D = peer_q.shape[0]
    n = keys.shape[0] * keys.shape[1]
    return pl.pallas_call(
        _keyproj_kernel,
        grid=(n,),
        in_specs=[pl.BlockSpec((1, N_KEYS, N_KEYS), lambda i: (i, 0, 0)),
                  pl.BlockSpec((D, N_KEYS), lambda i: (0, i))],
        out_specs=pl.BlockSpec((N_KEYS, D), lambda i: (i, 0)),
        out_shape=jax.ShapeDtypeStruct((n * N_KEYS, D), BF16),
        name="keyproj",
    )(keys.reshape(n, N_KEYS, N_KEYS), peer_q)


def _select_kernel(ht_ref, ws_ref, rank2_out, e2_out, b_out, f_out,
                   s_ref, v2_ref, cand_ref, bm_ref):
    s_ref[...] = _dot(ws_ref[...], ht_ref[...])
    outs = (rank2_out, e2_out, b_out, f_out)
    n_strips = s_ref.shape[1] // SELECT_STRIP

    def strip(idx, carry):
        head = idx // n_strips
        lanes = pl.ds(pl.multiple_of((idx % n_strips) * SELECT_STRIP, SELECT_STRIP), SELECT_STRIP)
        s1 = s_ref[pl.ds(pl.multiple_of(head * 2 * N_KEYS, N_KEYS), N_KEYS), lanes]
        s2 = s_ref[pl.ds(pl.multiple_of(head * 2 * N_KEYS + N_KEYS, N_KEYS), N_KEYS), lanes]
        maybe_tied = _select_strip_fast(s1, s2, outs, head, lanes, v2_ref, cand_ref, bm_ref)

        @pl.when(maybe_tied > float(PEER_TOPK))
        def _():
            _select_strip_exact(s1, s2, outs, head, lanes, v2_ref, cand_ref, bm_ref)

        return carry

    lax.fori_loop(0, SELECT_HEADS * n_strips, strip, 0)


def _select_call(h2t, ws):
    D, T = h2t.shape
    out = jax.ShapeDtypeStruct((PEER_HEADS, N_KEYS, T), F32)
    out16 = jax.ShapeDtypeStruct((PEER_HEADS, N_KEYS, T), BF16)
    ospec = pl.BlockSpec((SELECT_HEADS, N_KEYS, TQ), lambda i, h: (h, 0, i))
    return pl.pallas_call(
        _select_kernel,
        grid=(T // TQ, PEER_HEADS // SELECT_HEADS),
        in_specs=[pl.BlockSpec((D, TQ), lambda i, h: (0, i)),
                  pl.BlockSpec((SELECT_HEADS * 2 * N_KEYS, D), lambda i, h: (h, 0))],
        out_specs=[ospec] * 4,
        out_shape=[out16, out16, out, out],
        scratch_shapes=[pltpu.VMEM((SELECT_HEADS * 2 * N_KEYS, TQ), F32),
                        pltpu.VMEM((PEER_TOPK, SELECT_STRIP), F32),
                        pltpu.VMEM((N_CAND_PAD, SELECT_STRIP), F32),
                        pltpu.VMEM((PEER_TOPK, SELECT_STRIP), F32)],
        compiler_params=pltpu.CompilerParams(dimension_semantics=("arbitrary", "arbitrary"),
                                             vmem_limit_bytes=VMEM_LIMIT),
        name="select",
    )(h2t, ws)


def _transpose_cast_kernel(x_ref, o_ref):
    o_ref[...] = x_ref[0].T.astype(BF16)


def _transpose_cast_call(w, layer):
    _, n, d = w.shape
    return pl.pallas_call(
        _transpose_cast_kernel,
        grid=(n // EB,),
        in_specs=[pl.BlockSpec((1, EB, d), lambda i: (layer, i, 0))],
        out_specs=pl.BlockSpec((d, EB), lambda i: (0, i)),
        out_shape=jax.ShapeDtypeStruct((d, n), BF16),
        name="transpose_cast",
    )(w)


def _experts_kernel(ht_ref, u_ref, vt_ref, rank2_ref, e2_ref, b_ref, f_ref, x_ref, mod_ref, lnf_ref,
                    x_out, acc_ref, *, tiles_per_batch, final):
    e = pl.program_id(1)

    @pl.when(e == 0)
    def _():
        acc_ref[...] = jnp.zeros_like(acc_ref)

    tt = ht_ref.shape[1]

    gates = []
    for ii in range(FIRST_KEYS_PER_STEP):
        gate = None
        for h in range(PEER_HEADS):
            brow = jnp.broadcast_to(b_ref[h, e, ii:ii + 1, :], (N_KEYS, tt)).astype(BF16)
            frow = jnp.broadcast_to(f_ref[h, e, ii:ii + 1, :], (N_KEYS, tt)).astype(BF16)
            term = jnp.where(rank2_ref[h] <= brow, e2_ref[h], 0.0) * frow
            gate = term if gate is None else gate + term
        gates.append(gate)
    act = _dot(u_ref[...], ht_ref[...]).astype(BF16)
    gelu = 0.5 * act * (1.0 + lax.erf(act * (2.0 ** -0.5)))
    z = jnp.concatenate(gates, axis=0) * gelu
    acc_ref[...] += _dot(vt_ref[...], z)

    @pl.when(e == pl.num_programs(1) - 1)
    def _():
        m = mod_ref[0]
        x = x_ref[...] + m[5:6] * acc_ref[...].T
        if final:
            x = x * lax.rsqrt(jnp.mean(x * x, axis=-1, keepdims=True) + NORM_EPS) * lnf_ref[...]
        x_out[...] = x


def _experts_call(h2t, u_bf, vt_bf, rank2, e2, bsel, fsel, x2, modl, lnf_g, tiles_per_batch, final):
    D, T = h2t.shape
    NE = u_bf.shape[0]
    once = dict(pipeline_mode=pl.Buffered(1))
    sel = pl.BlockSpec((PEER_HEADS, N_KEYS, TT), lambda i, e: (0, 0, i), **once)
    bsel, fsel = (t.reshape(PEER_HEADS, N_KEYS // FIRST_KEYS_PER_STEP, FIRST_KEYS_PER_STEP, T)
                  for t in (bsel, fsel))
    sel1 = pl.BlockSpec(bsel.shape[:3] + (TT,), lambda i, e: (0, 0, 0, i), **once)
    tok = pl.BlockSpec((TT, D), lambda i, e: (i, 0))
    return pl.pallas_call(
        functools.partial(_experts_kernel, tiles_per_batch=tiles_per_batch, final=final),
        grid=(T // TT, NE // EB),
        in_specs=[pl.BlockSpec((D, TT), lambda i, e: (0, i), **once),
                  pl.BlockSpec((EB, D), lambda i, e: (e, 0)),
                  pl.BlockSpec((D, EB), lambda i, e: (0, e)),
                  sel, sel, sel1, sel1, pl.BlockSpec((TT, D), lambda i, e: (i, 0), **once),
                  pl.BlockSpec((1, 6, D), lambda i, e: (i // tiles_per_batch, 0, 0)),
                  pl.BlockSpec((1, D), lambda i, e: (0, 0))],
        out_specs=tok,
        out_shape=jax.ShapeDtypeStruct((T, D), F32),
        scratch_shapes=[pltpu.VMEM((D, TT), F32)],
        compiler_params=pltpu.CompilerParams(dimension_semantics=("arbitrary", "arbitrary"),
                                             vmem_limit_bytes=VMEM_LIMIT),
        name="experts",
    )(h2t, u_bf, vt_bf, rank2, e2, bsel, fsel, x2, modl, lnf_g)


def kernel(x, c, ada_w, ada_b, ln1_g, w_in, mu_shift, w0, w_up, a0, a_up, g_up, vres_down, vres_mu,
           vres_v0, vres_up, k_k, k_a, r_k, lnx_g, lnx_b, pool_w, pool_scale, w_out, ln2_g, peer_q,
           peer_keys, peer_u, peer_v, lnf_g):
    B, S, D = x.shape
    L = ada_w.shape[0]
    T = B * S
    assert S % TS == 0 and S % TM == 0 and T % TQ == 0 and T % TT == 0 and TT <= S
    assert B % RWKV_SEQS_PER_STEP == 0
    row = lambda t: t.reshape(1, -1)

    c_pad = jnp.zeros((8, D), F32).at[:B].set(c)
    mod = _ada_call(c_pad, ada_w, ada_b)[:, :B].reshape(L, B, 6, D)

    half = LORA_COLS // 2
    x2 = x.reshape(T, D)
    v_first = None
    for l in range(L):
        if l == 0:
            w_full = w_in[l].astype(BF16)
            vres = None
        else:
            mv = vres_down.shape[-1]
            w_full = jnp.concatenate(
                [w_in[l], vres_down[l - 1], jnp.zeros((D, VRES_PAD - mv), F32)], axis=1).astype(BF16)
            vres = (jnp.zeros((1, VRES_PAD), F32).at[:, :mv].set(vres_mu[l - 1]),
                    row(vres_v0[l - 1]),
                    jnp.zeros((VRES_PAD, RWKV_WIDTH), F32).at[:mv].set(vres_up[l - 1]),
                    v_first)
        lora_w = jnp.zeros((LORA_COLS, 2 * RWKV_WIDTH), F32)
        lora_w = lora_w.at[:half, :RWKV_WIDTH].set(w_up[l]).at[half:, RWKV_WIDTH:].set(a_up[l])
        r, k2, v, lw, kk, a, g, pool_in = _premix_call(
            x2, mod[l], row(ln1_g[l]), w_full, row(mu_shift[l]), row(w0[l]), row(a0[l]), lora_w,
            g_up[l], row(k_k[l]), row(k_a[l]), vres, S // TM)
        if l == 0:
            v_first = v
        y_rwkv = _rwkv_call(r, k2, v, lw, kk, a, g, row(r_k[l]), row(lnx_g[l]), row(lnx_b[l]), B)
        x_mid, h2t = _postmix_call(x2, y_rwkv, pool_in, mod[l], pool_w[l], row(pool_scale[l]),
                                   w_out[l].astype(BF16), row(ln2_g[l]), S // TM)
        ws = _keyproj_call(peer_keys[l], peer_q[l])
        rank2, e2, bsel, fsel = _select_call(h2t, ws)
        x2 = _experts_call(h2t, peer_u[l].astype(BF16), _transpose_cast_call(peer_v, l), rank2, e2, bsel, fsel,
                           x_mid, mod[l], row(lnf_g), S // TT, l == L - 1)
    return x2.reshape(B, S, D)
```

```python
import functools

import jax
import jax.numpy as jnp
from jax import lax
from jax.experimental import pallas as pl
from jax.experimental.pallas import tpu as pltpu

F32 = jnp.float32
BF16 = jnp.bfloat16
HIGHEST = lax.Precision.HIGHEST

HEAD_SIZE = 64
RWKV_WIDTH = 512
POOL_WIDTH = 512
POOL_GROUP = 128
POOL_WINDOWS = (2, 4, 8, 16)
MAX_WINDOW = 16
RWKV_COLS = 1792
LORA_COLS = 128
GATE_LORA = 128
VRES_PAD = 128
GN_EPS = 64e-5
NORM_EPS = 1e-6
N_KEYS = 128
PEER_HEADS = 8
PEER_TOPK = 16
CHUNK = 64
PAIR = 128
RWKV_SEQS_PER_STEP = 2

TM = 512
TS = 512
TQ = 1024
SELECT_HEADS = 2
SELECT_STRIP = 128
TT = 1024
EB = 1024
FIRST_KEYS_PER_STEP = EB // N_KEYS

VMEM_LIMIT = 56 * 1024 * 1024

_CAND_ROWS = [(a, PEER_TOPK // (a + 1)) for a in range(PEER_TOPK)]
_CAND_OFFS = []
_off = 0
for _a, _nb in _CAND_ROWS:
    _CAND_OFFS.append(_off)
    _off += _nb
N_CAND = _off
N_CAND_PAD = -(-N_CAND // 8) * 8


def _dot(a, b, dims=(((1,), (0,)), ((), ())), precision=None):
    return lax.dot_general(a, b, dims, precision=precision, preferred_element_type=F32)


_NT = (((1,), (1,)), ((), ()))
_TN = (((0,), (0,)), ((), ()))
_NN = (((1,), (0,)), ((), ()))


def _dot_split(a, b, pieces, split_lhs=True):
    rem = a if split_lhs else b
    acc = None
    for i in range(pieces):
        hi = rem.astype(BF16)
        part = _dot(hi, b, _NN) if split_lhs else _dot(a, hi, _NN)
        acc = part if acc is None else acc + part
        if i + 1 < pieces:
            rem = rem - hi.astype(F32)
    return acc


def _dot3(a, b):
    ah = a.astype(BF16)
    bh = b.astype(BF16)
    al = (a - ah.astype(F32)).astype(BF16)
    bl = (b - bh.astype(F32)).astype(BF16)
    return _dot(ah, bh) + (_dot(ah, bl) + _dot(al, bh))


def _group_ones(n, group):
    r = lax.broadcasted_iota(jnp.int32, (n, n), 0) // group
    c = lax.broadcasted_iota(jnp.int32, (n, n), 1) // group
    return (r == c).astype(F32)


def _rms_mod(x, g, shift, scale):
    y = x * lax.rsqrt(jnp.mean(x * x, axis=-1, keepdims=True) + NORM_EPS)
    return (y * g) * (1.0 + scale) + shift


def _ada_kernel(c_ref, w_ref, b_ref, o_ref):
    c = c_ref[...]
    ca = c * jax.nn.sigmoid(c)
    o_ref[0] = _dot(ca, w_ref[0], precision=HIGHEST) + b_ref[0]


def _ada_call(c_pad, ada_w, ada_b):
    L, D, D6 = ada_w.shape
    nb = D6 // D
    return pl.pallas_call(
        _ada_kernel,
        grid=(L, nb),
        in_specs=[
            pl.BlockSpec((8, D), lambda l, j: (0, 0)),
            pl.BlockSpec((1, D, D), lambda l, j: (l, 0, j)),
            pl.BlockSpec((1, 1, D), lambda l, j: (l, 0, j)),
        ],
        out_specs=pl.BlockSpec((1, 8, D), lambda l, j: (l, 0, j)),
        out_shape=jax.ShapeDtypeStruct((L, 8, D6), F32),
        name="adaln",
    )(c_pad, ada_w, ada_b.reshape(L, 1, D6))


def _premix_kernel(*refs, tiles_per_batch, has_vres):
    if has_vres:
        (x_ref, mod_ref, ln_ref, w_ref, mu_ref, w0_ref, a0_ref, lora_ref, gup_ref, kk_ref, ka_ref,
         vmu_ref, v0_ref, vup_ref, vfirst_ref,
         r_out, k_out, v_out, lw_out, kkn_out, a_out, g_out, p_out, carry_ref) = refs
    else:
        (x_ref, mod_ref, ln_ref, w_ref, mu_ref, w0_ref, a0_ref, lora_ref, gup_ref, kk_ref, ka_ref,
         r_out, k_out, v_out, lw_out, kkn_out, a_out, g_out, p_out, carry_ref) = refs
    i = pl.program_id(0)

    @pl.when(i % tiles_per_batch == 0)
    def _():
        carry_ref[...] = jnp.zeros_like(carry_ref)

    m = mod_ref[0]
    h = _rms_mod(x_ref[...], ln_ref[...], m[0:1], m[1:2])
    proj = _dot(h.astype(BF16), w_ref[...])
    tm = proj.shape[0]
    row0 = lax.broadcasted_iota(jnp.int32, (tm, 1), 0) == 0
    carry = carry_ref[...]
    carry_ref[...] = proj[tm - 1:tm, :]

    def shifted(lo, hi, mu):
        z = proj[:, lo:hi]
        prev = jnp.where(row0, carry[:, lo:hi], pltpu.roll(z, 1, 0))
        return z + (prev - z) * mu

    rw = shifted(0, RWKV_COLS, mu_ref[...])
    r = rw[:, 0:RWKV_WIDTH]
    k = rw[:, RWKV_WIDTH:2 * RWKV_WIDTH]
    v = rw[:, 2 * RWKV_WIDTH:3 * RWKV_WIDTH]
    lo = 3 * RWKV_WIDTH
    wa = rw[:, lo:lo + LORA_COLS]
    gd = rw[:, lo + LORA_COLS:lo + LORA_COLS + GATE_LORA]
    lane = lax.broadcasted_iota(jnp.int32, wa.shape, 1)
    wa = jnp.where(lane < LORA_COLS // 2, jnp.tanh(wa), wa)
    lora = _dot3(wa, lora_ref[...])
    zw = -(w0_ref[...] + lora[:, 0:RWKV_WIDTH])
    softplus = jnp.maximum(zw, 0.0) + jnp.log1p(jnp.exp(-jnp.abs(zw)))
    w = -softplus - 0.5
    a = jax.nn.sigmoid(a0_ref[...] + lora[:, RWKV_WIDTH:2 * RWKV_WIDTH])
    g = _dot3(jax.nn.sigmoid(gd), gup_ref[...])
    kk = k * kk_ref[...]
    ss = _dot_split(kk * kk, _group_ones(RWKV_WIDTH, HEAD_SIZE).astype(BF16), 3)
    kk = kk / jnp.maximum(jnp.sqrt(ss), 1e-12)
    k2 = k * (1.0 + (a - 1.0) * ka_ref[...])
    if has_vres:
        base = RWKV_COLS + POOL_WIDTH
        vd = shifted(base, base + VRES_PAD, vmu_ref[...])
        mixv = jax.nn.sigmoid(v0_ref[...] + _dot3(vd, vup_ref[...]))
        v = v + (vfirst_ref[...] - v) * mixv
    r_out[...] = r
    k_out[...] = k2
    v_out[...] = v
    lw_out[...] = -jnp.exp(w)
    kkn_out[...] = kk
    a_out[...] = a
    g_out[...] = g
    p_out[...] = proj[:, RWKV_COLS:RWKV_COLS + POOL_WIDTH]


def _premix_call(x2, modl, ln_g, w_full, mu, w0, a0, lora_w, g_up, k_k, k_a, vres, tiles_per_batch):
    T, D = x2.shape
    NC = w_full.shape[1]
    has_vres = vres is not None
    row = lambda n: pl.BlockSpec((1, n), lambda i: (0, 0))
    full = lambda a: pl.BlockSpec(a.shape, lambda i: (0,) * a.ndim)
    tok = lambda n: pl.BlockSpec((TM, n), lambda i: (i, 0))
    in_specs = [tok(D), pl.BlockSpec((1, 6, D), lambda i: (i // tiles_per_batch, 0, 0)), row(D),
                full(w_full), row(RWKV_COLS), row(RWKV_WIDTH), row(RWKV_WIDTH), full(lora_w), full(g_up),
                row(RWKV_WIDTH), row(RWKV_WIDTH)]
    args = [x2, modl, ln_g, w_full, mu, w0, a0, lora_w, g_up, k_k, k_a]
    if has_vres:
        vmu, v0, vup, vfirst = vres
        in_specs += [row(VRES_PAD), row(RWKV_WIDTH), full(vup), tok(RWKV_WIDTH)]
        args += [vmu, v0, vup, vfirst]
    out = jax.ShapeDtypeStruct((T, RWKV_WIDTH), F32)
    return pl.pallas_call(
        functools.partial(_premix_kernel, tiles_per_batch=tiles_per_batch, has_vres=has_vres),
        grid=(T // TM,),
        in_specs=in_specs,
        out_specs=[tok(RWKV_WIDTH)] * 8,
        out_shape=[out] * 8,
        scratch_shapes=[pltpu.VMEM((1, NC), F32)],
        compiler_params=pltpu.CompilerParams(dimension_semantics=("arbitrary",),
                                             vmem_limit_bytes=VMEM_LIMIT),
        name="premix",
    )(*args)


def _rwkv_kernel(r_ref, k_ref, v_ref, lw_ref, kk_ref, a_ref, g_ref, rk_ref, lng_ref, lnb_ref,
                 o_ref, state_ref, y_ref, ra_ref, yu_ref, xbk_ref, xv_ref, gc_ref):
    @pl.when(pl.program_id(1) == 0)
    def _():
        state_ref[...] = jnp.zeros_like(state_ref)

    n_seq, ts, width = r_ref.shape
    n_pairs = width // PAIR
    n_chunks = ts // CHUNK
    ti = lax.broadcasted_iota(jnp.int32, (CHUNK, CHUNK), 0)
    si = lax.broadcasted_iota(jnp.int32, (CHUNK, CHUNK), 1)
    tril = (ti >= si).astype(BF16)
    ri = lax.broadcasted_iota(jnp.int32, (2 * PAIR, 2 * PAIR), 0)
    ci = lax.broadcasted_iota(jnp.int32, (2 * PAIR, 2 * PAIR), 1)
    keep = (ri % CHUNK) + (ri >= PAIR).astype(jnp.int32) > (ci % CHUNK)
    ei = lax.broadcasted_iota(jnp.int32, (PAIR, PAIR), 0)
    ej = lax.broadcasted_iota(jnp.int32, (PAIR, PAIR), 1)
    eye = (ei == ej).astype(F32)
    lane = lax.broadcasted_iota(jnp.int32, (CHUNK, PAIR), 1)
    head0 = lane < HEAD_SIZE
    zeros_pp = jnp.zeros((PAIR, PAIR), BF16)

    def stack(t):
        return jnp.concatenate([jnp.where(head0, t, 0.0), jnp.where(head0, 0.0, t)], axis=0).astype(BF16)

    def pre_body(b, carry):
        chains = []
        for cc in range(n_chunks):
            c = b * n_chunks + cc
            rows = slice(cc * CHUNK, (cc + 1) * CHUNK)
            lw = lw_ref[b, rows, :]
            cum = _dot_split(tril, lw, 3, split_lhs=False)
            g_in = jnp.exp(cum)
            g_ex = jnp.exp(cum - lw)
            g_inv = jnp.exp(-cum)
            gc_ref[c] = g_in[CHUNK - 1:CHUNK, :]
            kk = kk_ref[b, rows, :]
            rt = r_ref[b, rows, :] * g_in
            kt = k_ref[b, rows, :] * g_inv
            bt = (kk * a_ref[b, rows, :]) * g_inv
            at = -kk * g_ex
            vv = v_ref[b, rows, :]
            for p in range(n_pairs):
                ls = slice(p * PAIR, (p + 1) * PAIR)
                xa, xb, xk, xr, xv = (stack(t[:, ls]) for t in (at, bt, kt, rt, vv))
                xbk = jnp.concatenate([xb, xk], axis=0)
                xbk_ref[c, p] = xbk
                xv_ref[c, p] = xv
                chains.append(dict(c=c, p=p, xa=xa, xr=xr, xv=xv, xbk=xbk))
        for ch in chains:
            big = _dot(jnp.concatenate([ch["xa"], ch["xr"]], axis=0), ch["xbk"], _NT)
            big = jnp.where(keep, big, 0.0)
            ch["lab"] = big[0:PAIR, 0:PAIR]
            ch["lak"] = big[0:PAIR, PAIR:2 * PAIR].astype(BF16)
            ch["lrbk"] = big[PAIR:2 * PAIR, :].astype(BF16)
        for ch in chains:
            lb = ch["lab"].astype(BF16)
            ch["pw"] = _dot(lb, lb)
            ch["lakv"] = _dot(ch["lak"], ch["xv"]).astype(BF16)
            ch["inv"] = eye + ch["lab"]
        for i in range(1, 6):
            for ch in chains:
                pb = ch["pw"].astype(BF16)
                sb = ch["inv"].astype(BF16)
                if i < 5:
                    res = _dot(pb, jnp.concatenate([pb, sb], axis=1))
                    ch["pw"] = res[:, 0:PAIR]
                    ch["inv"] = ch["inv"] + res[:, PAIR:2 * PAIR]
                else:
                    ch["inv"] = ch["inv"] + _dot(pb, sb)
        for ch in chains:
            au = _dot(ch["inv"].astype(BF16), jnp.concatenate([ch["xa"], ch["lakv"]], axis=1))
            ch["a2"] = au[:, 0:PAIR].astype(BF16)
            ch["uv"] = au[:, PAIR:2 * PAIR]
        for ch in chains:
            rhs = jnp.concatenate(
                [jnp.concatenate([ch["a2"], ch["uv"].astype(BF16)], axis=1),
                 jnp.concatenate([zeros_pp, ch["xv"]], axis=1)], axis=0)
            ry = _dot(ch["lrbk"], rhs)
            r2 = ch["xr"].astype(F32) + ry[:, 0:PAIR]
            ra_ref[ch["c"], ch["p"]] = jnp.concatenate([r2.astype(BF16), ch["a2"]], axis=0)
            yu_ref[ch["c"], ch["p"]] = jnp.concatenate([ry[:, PAIR:2 * PAIR], ch["uv"]], axis=0)
        return carry

    lax.fori_loop(0, n_seq, pre_body, 0)

    def seq_body(cc, carry):
        rows = pl.ds(pl.multiple_of(cc * CHUNK, CHUNK), CHUNK)
        chains = [(b, p, b * n_chunks + cc) for b in range(n_seq) for p in range(n_pairs)]
        res = [_dot(ra_ref[c, p], state_ref[b, p].astype(BF16), _NT) + yu_ref[c, p] for b, p, c in chains]
        for (b, p, c), res_bp in zip(chains, res):
            ls = slice(p * PAIR, (p + 1) * PAIR)
            y_st = res_bp[0:PAIR]
            u_st = res_bp[PAIR:2 * PAIR].astype(BF16)
            st = _dot(jnp.concatenate([u_st, xv_ref[c, p]], axis=0), xbk_ref[c, p], _TN)
            state_ref[b, p] = (state_ref[b, p] + st) * gc_ref[c][:, ls]
            y_ref[b, rows, ls] = y_st[0:CHUNK] + y_st[CHUNK:2 * CHUNK]
        return carry

    lax.fori_loop(0, n_chunks, seq_body, 0)

    ones = _group_ones(RWKV_WIDTH, HEAD_SIZE).astype(BF16)
    flat = lambda ref: ref[...].reshape(n_seq * ts, width)
    y = flat(y_ref)
    mean = _dot_split(y, ones, 2) * (1.0 / HEAD_SIZE)
    d = y - mean
    var = _dot_split(d * d, ones, 2) * (1.0 / HEAD_SIZE)
    yn = d * lax.rsqrt(var + GN_EPS) * lng_ref[...] + lnb_ref[...]
    bonus = _dot_split(flat(r_ref) * flat(k_ref) * rk_ref[...], ones, 2) * flat(v_ref)
    o_ref[...] = ((yn + bonus) * flat(g_ref)).reshape(n_seq, ts, width)


def _rwkv_call(r, k, v, lw, kk, a, g, r_k, lnx_g, lnx_b, batch):
    T, W = r.shape
    seq = T // batch
    ts = TS // RWKV_SEQS_PER_STEP
    n_chunks, n_pairs = TS // CHUNK, W // PAIR
    tok = pl.BlockSpec((RWKV_SEQS_PER_STEP, ts, W), lambda b, j: (b, j, 0))
    row = pl.BlockSpec((1, W), lambda b, j: (0, 0))
    per_seq = lambda t: t.reshape(batch, seq, W)
    return pl.pallas_call(
        _rwkv_kernel,
        grid=(batch // RWKV_SEQS_PER_STEP, seq // ts),
        in_specs=[tok] * 7 + [row] * 3,
        out_specs=tok,
        out_shape=jax.ShapeDtypeStruct((batch, seq, W), F32),
        scratch_shapes=[pltpu.VMEM((RWKV_SEQS_PER_STEP, n_pairs, PAIR, PAIR), F32),
                        pltpu.VMEM((RWKV_SEQS_PER_STEP, ts, W), F32),
                        pltpu.VMEM((n_chunks, n_pairs, 2 * PAIR, PAIR), BF16),
                        pltpu.VMEM((n_chunks, n_pairs, 2 * PAIR, PAIR), F32),
                        pltpu.VMEM((n_chunks, n_pairs, 2 * PAIR, PAIR), BF16),
                        pltpu.VMEM((n_chunks, n_pairs, PAIR, PAIR), BF16),
                        pltpu.VMEM((n_chunks, 1, W), F32)],
        compiler_params=pltpu.CompilerParams(dimension_semantics=("arbitrary", "arbitrary"),
                                             vmem_limit_bytes=VMEM_LIMIT),
        name="rwkv",
    )(*(per_seq(t) for t in (r, k, v, lw, kk, a, g)), r_k, lnx_g, lnx_b).reshape(T, W)


def _postmix_kernel(x_ref, y_ref, p_ref, mod_ref, pw_ref, ps_ref, wo_ref, ln_ref,
                    x_out, ht_out, carry_ref, *, tiles_per_batch):
    i = pl.program_id(0)
    j = i % tiles_per_batch

    @pl.when(j == 0)
    def _():
        carry_ref[...] = jnp.zeros_like(carry_ref)

    m = mod_ref[0]
    p = p_ref[...]
    tm = p.shape[0]
    ext = jnp.concatenate([carry_ref[...], p], axis=0)
    carry_ref[...] = p[tm - MAX_WINDOW:tm, :]
    pos = (j * tm + 1 + lax.broadcasted_iota(jnp.int32, (tm, 1), 0)).astype(F32)
    pooled = []
    for gi, win in enumerate(POOL_WINDOWS):
        ls = slice(gi * POOL_GROUP, (gi + 1) * POOL_GROUP)
        wsum = ext[:, ls]
        span = 1
        while span < win:
            wsum = wsum + pltpu.roll(wsum, span, 0)
            span *= 2
        mean = wsum[MAX_WINDOW:, :] / jnp.minimum(pos, float(win))
        d = mean - p[:, ls]
        pooled.append(_dot3(d, pw_ref[gi]))
    y_pool = jnp.concatenate(pooled, axis=1) * ps_ref[...]
    mix = (_dot(y_ref[...].astype(BF16), wo_ref[0:RWKV_WIDTH, :])
           + _dot(y_pool.astype(BF16), wo_ref[RWKV_WIDTH:RWKV_WIDTH + POOL_WIDTH, :]))
    x = x_ref[...] + m[2:3] * mix
    x_out[...] = x
    h2 = _rms_mod(x, ln_ref[...], m[3:4], m[4:5])
    ht_out[...] = h2.T.astype(BF16)


def _postmix_call(x2, y_rwkv, pool_in, modl, pool_w, pool_scale, w_out, ln_g, tiles_per_batch):
    T, D = x2.shape
    row = lambda n: pl.BlockSpec((1, n), lambda i: (0, 0))
    full = lambda a: pl.BlockSpec(a.shape, lambda i: (0,) * a.ndim)
    tok = lambda n: pl.BlockSpec((TM, n), lambda i: (i, 0))
    return pl.pallas_call(
        functools.partial(_postmix_kernel, tiles_per_batch=tiles_per_batch),
        grid=(T // TM,),
        in_specs=[tok(D), tok(RWKV_WIDTH), tok(POOL_WIDTH),
                  pl.BlockSpec((1, 6, D), lambda i: (i // tiles_per_batch, 0, 0)),
                  full(pool_w), row(POOL_WIDTH), full(w_out), row(D)],
        out_specs=[tok(D), pl.BlockSpec((D, TM), lambda i: (0, i))],
        out_shape=[jax.ShapeDtypeStruct((T, D), F32), jax.ShapeDtypeStruct((D, T), BF16)],
        scratch_shapes=[pltpu.VMEM((MAX_WINDOW, POOL_WIDTH), F32)],
        compiler_params=pltpu.CompilerParams(dimension_semantics=("arbitrary",),
                                             vmem_limit_bytes=VMEM_LIMIT),
        name="postmix",
    )(x2, y_rwkv, pool_in, modl, pool_w, pool_scale, w_out, ln_g)


def _extract16(cur, mark, marks, tie_break):
    n, w = cur.shape
    iota = lax.broadcasted_iota(jnp.int32, (n, w), 0).astype(F32)
    vals = []
    for j in range(PEER_TOPK):
        mx = jnp.max(cur, axis=0, keepdims=True)
        hit = cur == mx
        if tie_break:
            first = jnp.min(jnp.where(hit, iota, float(n)), axis=0, keepdims=True)
            hit = iota == first
        mark = jnp.where(hit, marks[j], mark)
        cur = jnp.where(hit, -jnp.inf, cur)
        vals.append(mx)
    removed = jnp.sum((cur == -jnp.inf).astype(F32), axis=0, keepdims=True)
    return mark, vals, removed


def _pair_stage(vals1, v2_ref, cand_ref, bm_ref, tie_break):
    w = v2_ref.shape[1]
    for (a, nb), off in zip(_CAND_ROWS, _CAND_OFFS):
        cand_ref[off:off + nb, :] = vals1[a] + v2_ref[0:nb, :]
    cand_ref[N_CAND:N_CAND_PAD, :] = jnp.full((N_CAND_PAD - N_CAND, w), -jnp.inf, F32)
    cand = cand_ref[...]
    chosen, _, removed = _extract16(cand, jnp.zeros(cand.shape, F32), [1.0] * PEER_TOPK, tie_break)
    z = jnp.sum(chosen * jnp.exp(cand - cand[0:1, :]), axis=0, keepdims=True)
    cand_ref[...] = chosen
    for (a, nb), off in zip(_CAND_ROWS, _CAND_OFFS):
        bm_ref[a:a + 1, :] = jnp.sum(cand_ref[off:off + nb, :], axis=0, keepdims=True)
    return z, removed - float(N_CAND_PAD - N_CAND)


def _select_strip_exact(s1, s2, outs, head, lanes, v2_ref, cand_ref, bm_ref):
    rank2_out, e2_out, b_out, f_out = outs
    ranks = [float(j + 1) for j in range(PEER_TOPK)]
    unranked = jnp.full(s1.shape, 99.0, F32)
    rank1, vals1, _ = _extract16(s1, unranked, ranks, True)
    rank2, vals2, _ = _extract16(s2, unranked, ranks, True)
    for j in range(PEER_TOPK):
        v2_ref[j:j + 1, :] = vals2[j]
    z, _ = _pair_stage(vals1, v2_ref, cand_ref, bm_ref, True)
    bsel = jnp.zeros(s1.shape, F32)
    for a in range(PEER_TOPK):
        bsel = jnp.where(rank1 == ranks[a], bm_ref[a:a + 1, :], bsel)
    rank2_out[head, :, lanes] = rank2.astype(BF16)
    e2_out[head, :, lanes] = jnp.exp(s2 - vals2[0]).astype(BF16)
    b_out[head, :, lanes] = bsel
    f_out[head, :, lanes] = jnp.exp(s1 - vals1[0]) / z


def _oddeven_merge(lo, hi, r):
    step = r * 2
    if step < hi - lo:
        yield from _oddeven_merge(lo, hi, step)
        yield from _oddeven_merge(lo + r, hi, step)
        yield from [(i, i + r) for i in range(lo + r, hi - r, step)]
    else:
        yield (lo, lo + r)


def _oddeven_sort(lo, hi):
    if hi - lo >= 1:
        mid = lo + (hi - lo) // 2
        yield from _oddeven_sort(lo, mid)
        yield from _oddeven_sort(mid + 1, hi)
        yield from _oddeven_merge(lo, hi, 1)


def _exchange(a, i, j):
    a[i], a[j] = jnp.maximum(a[i], a[j]), jnp.minimum(a[i], a[j])


def _sorted_top16(slabs):
    a = list(slabs)
    for i, j in _oddeven_sort(0, len(a) - 1):
        _exchange(a, i, j)
    for shift in (4, 2, 1):
        other = [pltpu.roll(t, shift, 0) for t in a]
        if len(a) < PEER_TOPK:
            a = a + other[::-1]
        else:
            a = [jnp.maximum(a[j], other[PEER_TOPK - 1 - j]) for j in range(PEER_TOPK)]
        d = PEER_TOPK // 2
        while d >= 1:
            for i in range(PEER_TOPK):
                if i & d == 0:
                    _exchange(a, i, i + d)
            d //= 2
    return a


def _count_above(x, v):
    c1 = v[7] > x
    c2 = jnp.where(c1, v[11], v[3]) > x
    c3 = jnp.where(c1, jnp.where(c2, v[13], v[9]), jnp.where(c2, v[5], v[1])) > x
    c4 = jnp.where(c1,
                   jnp.where(c2, jnp.where(c3, v[14], v[12]), jnp.where(c3, v[10], v[8])),
                   jnp.where(c2, jnp.where(c3, v[6], v[4]), jnp.where(c3, v[2], v[0]))) > x
    return c1, c2, c3, c4


def _pick16(masks, table):
    t = list(table)
    for c in reversed(masks):
        t = [jnp.where(c, t[2 * k + 1], t[2 * k]) for k in range(len(t) // 2)]
    return t[0]


def _select_strip_fast(s1, s2, outs, head, lanes, v2_ref, cand_ref, bm_ref):
    rank2_out, e2_out, b_out, f_out = outs
    w = s1.shape[1]
    n_slabs = s1.shape[0] // 8
    slabs1 = [s1[8 * k:8 * k + 8, :] for k in range(n_slabs)]
    slabs2 = [s2[8 * k:8 * k + 8, :] for k in range(n_slabs)]
    top1 = _sorted_top16(slabs1)
    top2 = _sorted_top16(slabs2)
    for j in range(PEER_TOPK):
        v2_ref[j:j + 1, :] = top2[j][0:1, :]
    for (a, nb), off in zip(_CAND_ROWS, _CAND_OFFS):
        cand_ref[off:off + nb, :] = top1[a][0:1, :] + v2_ref[0:nb, :]
    cand_ref[N_CAND:N_CAND_PAD, :] = jnp.full((N_CAND_PAD - N_CAND, w), -jnp.inf, F32)
    cand = cand_ref[...]
    cand_slabs = [cand[8 * k:8 * k + 8, :] for k in range(N_CAND_PAD // 8)]
    cand_slabs.append(jnp.full((8, w), -jnp.inf, F32))
    tau = _sorted_top16(cand_slabs)[PEER_TOPK - 1]
    chosen = cand >= tau[0:1, :]
    z = jnp.sum(jnp.where(chosen, jnp.exp(cand - cand[0:1, :]), 0.0), axis=0, keepdims=True)
    removed = jnp.sum(jnp.where(chosen, 1.0, 0.0), axis=0, keepdims=True)
    counts = []
    for a, nb in _CAND_ROWS:
        hits = [jnp.where(top1[a] + top2[b] >= tau, 1.0, 0.0) for b in range(nb)]
        counts.append(functools.reduce(lambda p, q: p + q, hits))
    zb = jnp.broadcast_to(z, (8, w))
    suspect = jnp.zeros((8, w), F32)
    for j in range(PEER_TOPK - 1):
        suspect = jnp.where(top1[j] == top1[j + 1], 99.0, suspect)
        suspect = jnp.where(top2[j] == top2[j + 1], 99.0, suspect)
    in_top1 = jnp.zeros((8, w), F32)
    in_top2 = jnp.zeros((8, w), F32)
    rank2, e2, bsel, fsel = [], [], [], []
    for k in range(n_slabs):
        x1, x2 = slabs1[k], slabs2[k]
        keep1 = x1 >= top1[PEER_TOPK - 1]
        keep2 = x2 >= top2[PEER_TOPK - 1]
        in_top1 = in_top1 + jnp.where(keep1, 1.0, 0.0)
        in_top2 = in_top2 + jnp.where(keep2, 1.0, 0.0)
        bsel.append(jnp.where(keep1, _pick16(_count_above(x1, top1), counts), 0.0))
        c1, c2, c3, c4 = _count_above(x2, top2)
        rank = (1.0 + jnp.where(c1, 8.0, 0.0)) + (jnp.where(c2, 4.0, 0.0)
                                                  + (jnp.where(c3, 2.0, 0.0) + jnp.where(c4, 1.0, 0.0)))
        rank2.append(jnp.where(keep2, rank, 99.0))
        e2.append(jnp.exp(x2 - top2[0]))
        fsel.append(jnp.exp(x1 - top1[0]) / zb)
    rank2_out[head, :, lanes] = jnp.concatenate(rank2, axis=0).astype(BF16)
    e2_out[head, :, lanes] = jnp.concatenate(e2, axis=0).astype(BF16)
    b_out[head, :, lanes] = jnp.concatenate(bsel, axis=0)
    f_out[head, :, lanes] = jnp.concatenate(fsel, axis=0)
    kept = jnp.maximum(jnp.sum(in_top1, axis=0, keepdims=True), jnp.sum(in_top2, axis=0, keepdims=True))
    return jnp.maximum(jnp.max(jnp.maximum(kept, removed)), jnp.max(suspect))


def _keyproj_kernel(keys_ref, q_ref, o_ref):
    o_ref[...] = _dot(keys_ref[0], q_ref[...], _NT, precision=HIGHEST).astype(BF16)


def _keyproj_call(keys, peer_q):
    D = peer_q.shape[0]
    n = keys.shape[0] * keys.shape[1]
    return pl.pallas_call(
        _keyproj_kernel,
        grid=(n,),
        in_specs=[pl.BlockSpec((1, N_KEYS, N_KEYS), lambda i: (i, 0, 0)),
                  pl.BlockSpec((D, N_KEYS), lambda i: (0, i))],
        out_specs=pl.BlockSpec((N_KEYS, D), lambda i: (i, 0)),
        out_shape=jax.ShapeDtypeStruct((n * N_KEYS, D), BF16),
        name="keyproj",
    )(keys.reshape(n, N_KEYS, N_KEYS), peer_q)


def _select_kernel(ht_ref, ws_ref, rank2_out, e2_out, b_out, f_out,
                   s_ref, v2_ref, cand_ref, bm_ref):
    s_ref[...] = _dot(ws_ref[...], ht_ref[...])
    outs = (rank2_out, e2_out, b_out, f_out)
    n_strips = s_ref.shape[1] // SELECT_STRIP

    def strip(idx, carry):
        head = idx // n_strips
        lanes = pl.ds(pl.multiple_of((idx % n_strips) * SELECT_STRIP, SELECT_STRIP), SELECT_STRIP)
        s1 = s_ref[pl.ds(pl.multiple_of(head * 2 * N_KEYS, N_KEYS), N_KEYS), lanes]
        s2 = s_ref[pl.ds(pl.multiple_of(head * 2 * N_KEYS + N_KEYS, N_KEYS), N_KEYS), lanes]
        maybe_tied = _select_strip_fast(s1, s2, outs, head, lanes, v2_ref, cand_ref, bm_ref)

        @pl.when(maybe_tied > float(PEER_TOPK))
        def _():
            _select_strip_exact(s1, s2, outs, head, lanes, v2_ref, cand_ref, bm_ref)

        return carry

    lax.fori_loop(0, SELECT_HEADS * n_strips, strip, 0)


def _select_call(h2t, ws):
    D, T = h2t.shape
    out = jax.ShapeDtypeStruct((PEER_HEADS, N_KEYS, T), F32)
    out16 = jax.ShapeDtypeStruct((PEER_HEADS, N_KEYS, T), BF16)
    ospec = pl.BlockSpec((SELECT_HEADS, N_KEYS, TQ), lambda i, h: (h, 0, i))
    return pl.pallas_call(
        _select_kernel,
        grid=(T // TQ, PEER_HEADS // SELECT_HEADS),
        in_specs=[pl.BlockSpec((D, TQ), lambda i, h: (0, i)),
                  pl.BlockSpec((SELECT_HEADS * 2 * N_KEYS, D), lambda i, h: (h, 0))],
        out_specs=[ospec] * 4,
        out_shape=[out16, out16, out, out],
        scratch_shapes=[pltpu.VMEM((SELECT_HEADS * 2 * N_KEYS, TQ), F32),
                        pltpu.VMEM((PEER_TOPK, SELECT_STRIP), F32),
                        pltpu.VMEM((N_CAND_PAD, SELECT_STRIP), F32),
                        pltpu.VMEM((PEER_TOPK, SELECT_STRIP), F32)],
        compiler_params=pltpu.CompilerParams(dimension_semantics=("arbitrary", "arbitrary"),
                                             vmem_limit_bytes=VMEM_LIMIT),
        name="select",
    )(h2t, ws)


def _transpose_cast_kernel(x_ref, o_ref):
    o_ref[...] = x_ref[0].T.astype(BF16)


def _transpose_cast_call(w, layer):
    _, n, d = w.shape
    return pl.pallas_call(
        _transpose_cast_kernel,
        grid=(n // EB,),
        in_specs=[pl.BlockSpec((1, EB, d), lambda i: (layer, i, 0))],
        out_specs=pl.BlockSpec((d, EB), lambda i: (0, i)),
        out_shape=jax.ShapeDtypeStruct((d, n), BF16),
        name="transpose_cast",
    )(w)


def _experts_kernel(ht_ref, u_ref, vt_ref, rank2_ref, e2_ref, b_ref, f_ref, x_ref, mod_ref, lnf_ref,
                    x_out, acc_ref, *, tiles_per_batch, final):
    e = pl.program_id(1)

    @pl.when(e == 0)
    def _():
        acc_ref[...] = jnp.zeros_like(acc_ref)

    tt = ht_ref.shape[1]

    gates = []
    for ii in range(FIRST_KEYS_PER_STEP):
        gate = None
        for h in range(PEER_HEADS):
            brow = jnp.broadcast_to(b_ref[h, 0, ii:ii + 1, :], (N_KEYS, tt)).astype(BF16)
            frow = jnp.broadcast_to(f_ref[h, 0, ii:ii + 1, :], (N_KEYS, tt)).astype(BF16)
            term = jnp.where(rank2_ref[h] <= brow, e2_ref[h], 0.0) * frow
            gate = term if gate is None else gate + term
        gates.append(gate)
    act = _dot(u_ref[...], ht_ref[...]).astype(BF16)
    gelu = 0.5 * act * (1.0 + lax.erf(act * (2.0 ** -0.5)))
    z = jnp.concatenate(gates, axis=0) * gelu
    acc_ref[...] += _dot(vt_ref[...], z)

    @pl.when(e == pl.num_programs(1) - 1)
    def _():
        m = mod_ref[0]
        x = x_ref[...] + m[5:6] * acc_ref[...].T
        if final:
            x = x * lax.rsqrt(jnp.mean(x * x, axis=-1, keepdims=True) + NORM_EPS) * lnf_ref[...]
        x_out[...] = x


def _experts_call(h2t, u_bf, vt_bf, rank2, e2, bsel, fsel, x2, modl, lnf_g, tiles_per_batch, final):
    D, T = h2t.shape
    NE = u_bf.shape[0]
    sel = pl.BlockSpec((PEER_HEADS, N_KEYS, TT), lambda i, e: (0, 0, i))
    bsel, fsel = (t.reshape(PEER_HEADS, N_KEYS // FIRST_KEYS_PER_STEP, FIRST_KEYS_PER_STEP, T)
                  for t in (bsel, fsel))
    sel1 = pl.BlockSpec((PEER_HEADS, 1, FIRST_KEYS_PER_STEP, TT), lambda i, e: (0, e, 0, i))
    tok = pl.BlockSpec((TT, D), lambda i, e: (i, 0))
    return pl.pallas_call(
        functools.partial(_experts_kernel, tiles_per_batch=tiles_per_batch, final=final),
        grid=(T // TT, NE // EB),
        in_specs=[pl.BlockSpec((D, TT), lambda i, e: (0, i)),
                  pl.BlockSpec((EB, D), lambda i, e: (e, 0)),
                  pl.BlockSpec((D, EB), lambda i, e: (0, e)),
                  sel, sel, sel1, sel1, tok,
                  pl.BlockSpec((1, 6, D), lambda i, e: (i // tiles_per_batch, 0, 0)),
                  pl.BlockSpec((1, D), lambda i, e: (0, 0))],
        out_specs=tok,
        out_shape=jax.ShapeDtypeStruct((T, D), F32),
        scratch_shapes=[pltpu.VMEM((D, TT), F32)],
        compiler_params=pltpu.CompilerParams(dimension_semantics=("arbitrary", "arbitrary"),
                                             vmem_limit_bytes=VMEM_LIMIT),
        name="experts",
    )(h2t, u_bf, vt_bf, rank2, e2, bsel, fsel, x2, modl, lnf_g)


def kernel(x, c, ada_w, ada_b, ln1_g, w_in, mu_shift, w0, w_up, a0, a_up, g_up, vres_down, vres_mu,
           vres_v0, vres_up, k_k, k_a, r_k, lnx_g, lnx_b, pool_w, pool_scale, w_out, ln2_g, peer_q,
           peer_keys, peer_u, peer_v, lnf_g):
    B, S, D = x.shape
    L = ada_w.shape[0]
    T = B * S
    assert S % TS == 0 and S % TM == 0 and T % TQ == 0 and T % TT == 0 and TT <= S
    assert B % RWKV_SEQS_PER_STEP == 0
    row = lambda t: t.reshape(1, -1)

    c_pad = jnp.zeros((8, D), F32).at[:B].set(c)
    mod = _ada_call(c_pad, ada_w, ada_b)[:, :B].reshape(L, B, 6, D)

    half = LORA_COLS // 2
    x2 = x.reshape(T, D)
    v_first = None
    for l in range(L):
        if l == 0:
            w_full = w_in[l].astype(BF16)
            vres = None
        else:
            mv = vres_down.shape[-1]
            w_full = jnp.concatenate(
                [w_in[l], vres_down[l - 1], jnp.zeros((D, VRES_PAD - mv), F32)], axis=1).astype(BF16)
            vres = (jnp.zeros((1, VRES_PAD), F32).at[:, :mv].set(vres_mu[l - 1]),
                    row(vres_v0[l - 1]),
                    jnp.zeros((VRES_PAD, RWKV_WIDTH), F32).at[:mv].set(vres_up[l - 1]),
                    v_first)
        lora_w = jnp.zeros((LORA_COLS, 2 * RWKV_WIDTH), F32)
        lora_w = lora_w.at[:half, :RWKV_WIDTH].set(w_up[l]).at[half:, RWKV_WIDTH:].set(a_up[l])
        r, k2, v, lw, kk, a, g, pool_in = _premix_call(
            x2, mod[l], row(ln1_g[l]), w_full, row(mu_shift[l]), row(w0[l]), row(a0[l]), lora_w,
            g_up[l], row(k_k[l]), row(k_a[l]), vres, S // TM)
        if l == 0:
            v_first = v
        y_rwkv = _rwkv_call(r, k2, v, lw, kk, a, g, row(r_k[l]), row(lnx_g[l]), row(lnx_b[l]), B)
        x_mid, h2t = _postmix_call(x2, y_rwkv, pool_in, mod[l], pool_w[l], row(pool_scale[l]),
                                   w_out[l].astype(BF16), row(ln2_g[l]), S // TM)
        ws = _keyproj_call(peer_keys[l], peer_q[l])
        rank2, e2, bsel, fsel = _select_call(h2t, ws)
        x2 = _experts_call(h2t, peer_u[l].astype(BF16), _transpose_cast_call(peer_v, l), rank2, e2, bsel, fsel,
                           x_mid, mod[l], row(lnf_g), S // TT, l == L - 1)
    return x2.reshape(B, S, D)
```

```python
import functools

import jax
import jax.numpy as jnp
from jax import lax
from jax.experimental import pallas as pl
from jax.experimental.pallas import tpu as pltpu

F32 = jnp.float32
BF16 = jnp.bfloat16
HIGHEST = lax.Precision.HIGHEST

SUBLANES = 8
HEAD_SIZE = 64
RWKV_WIDTH = 512
POOL_WIDTH = 512
POOL_GROUP = 128
POOL_WINDOWS = (2, 4, 8, 16)
MAX_WINDOW = 16
RWKV_COLS = 1792
LORA_COLS = 128
GATE_LORA = 128
VRES_PAD = 128
GN_EPS = 64e-5
NORM_EPS = 1e-6
N_KEYS = 128
PEER_HEADS = 8
PEER_TOPK = 16
CHUNK = 64
PAIR = 128
RWKV_SEQS_PER_STEP = 2

TM = 512
TS = 512
TQ = 1024
SELECT_HEADS = 2
SELECT_STRIP = 128
TT = 1024
EB = 1024
FIRST_KEYS_PER_STEP = EB // N_KEYS

VMEM_LIMIT = 56 * 1024 * 1024

_CAND_ROWS = [(a, PEER_TOPK // (a + 1)) for a in range(PEER_TOPK)]
_CAND_OFFS = []
_off = 0
for _a, _nb in _CAND_ROWS:
    _CAND_OFFS.append(_off)
    _off += _nb
N_CAND = _off
N_CAND_PAD = -(-N_CAND // SUBLANES) * SUBLANES


def _dot(a, b, dims=(((1,), (0,)), ((), ())), precision=None):
    return lax.dot_general(a, b, dims, precision=precision, preferred_element_type=F32)


_NT = (((1,), (1,)), ((), ()))
_TN = (((0,), (0,)), ((), ()))
_NN = (((1,), (0,)), ((), ()))


def _dot_split(a, b, pieces, split_lhs=True):
    rem = a if split_lhs else b
    acc = None
    for i in range(pieces):
        hi = rem.astype(BF16)
        part = _dot(hi, b, _NN) if split_lhs else _dot(a, hi, _NN)
        acc = part if acc is None else acc + part
        if i + 1 < pieces:
            rem = rem - hi.astype(F32)
    return acc


def _dot3(a, b):
    ah = a.astype(BF16)
    bh = b.astype(BF16)
    al = (a - ah.astype(F32)).astype(BF16)
    bl = (b - bh.astype(F32)).astype(BF16)
    return _dot(ah, bh) + (_dot(ah, bl) + _dot(al, bh))


def _group_ones(n, group):
    r = lax.broadcasted_iota(jnp.int32, (n, n), 0) // group
    c = lax.broadcasted_iota(jnp.int32, (n, n), 1) // group
    return (r == c).astype(F32)


def _rms_mod(x, g, shift, scale):
    y = x * lax.rsqrt(jnp.mean(x * x, axis=-1, keepdims=True) + NORM_EPS)
    return (y * g) * (1.0 + scale) + shift


def _ada_kernel(c_ref, w_ref, b_ref, o_ref):
    c = c_ref[...]
    ca = c * jax.nn.sigmoid(c)
    o_ref[0] = _dot(ca, w_ref[0], precision=HIGHEST) + b_ref[0]


def _ada_call(c_pad, ada_w, ada_b):
    L, D, D6 = ada_w.shape
    nb = D6 // D
    return pl.pallas_call(
        _ada_kernel,
        grid=(L, nb),
        in_specs=[
            pl.BlockSpec((SUBLANES, D), lambda l, j: (0, 0)),
            pl.BlockSpec((1, D, D), lambda l, j: (l, 0, j)),
            pl.BlockSpec((1, 1, D), lambda l, j: (l, 0, j)),
        ],
        out_specs=pl.BlockSpec((1, SUBLANES, D), lambda l, j: (l, 0, j)),
        out_shape=jax.ShapeDtypeStruct((L, SUBLANES, D6), F32),
        name="adaln",
    )(c_pad, ada_w, ada_b.reshape(L, 1, D6))


def _premix_kernel(*refs, tiles_per_batch, has_vres):
    if has_vres:
        (x_ref, mod_ref, ln_ref, w_ref, mu_ref, w0_ref, a0_ref, lora_ref, gup_ref, kk_ref, ka_ref,
         vmu_ref, v0_ref, vup_ref, vfirst_ref,
         r_out, k_out, v_out, lw_out, kkn_out, a_out, g_out, p_out, carry_ref) = refs
    else:
        (x_ref, mod_ref, ln_ref, w_ref, mu_ref, w0_ref, a0_ref, lora_ref, gup_ref, kk_ref, ka_ref,
         r_out, k_out, v_out, lw_out, kkn_out, a_out, g_out, p_out, carry_ref) = refs
    i = pl.program_id(0)

    @pl.when(i % tiles_per_batch == 0)
    def _():
        carry_ref[...] = jnp.zeros_like(carry_ref)

    m = mod_ref[0]
    h = _rms_mod(x_ref[...], ln_ref[...], m[0:1], m[1:2])
    proj = _dot(h.astype(BF16), w_ref[...])
    tm = proj.shape[0]
    row0 = lax.broadcasted_iota(jnp.int32, (tm, 1), 0) == 0
    carry = carry_ref[...]
    carry_ref[...] = proj[tm - 1:tm, :]

    def shifted(lo, hi, mu):
        z = proj[:, lo:hi]
        prev = jnp.where(row0, carry[:, lo:hi], pltpu.roll(z, 1, 0))
        return z + (prev - z) * mu

    rw = shifted(0, RWKV_COLS, mu_ref[...])
    r = rw[:, 0:RWKV_WIDTH]
    k = rw[:, RWKV_WIDTH:2 * RWKV_WIDTH]
    v = rw[:, 2 * RWKV_WIDTH:3 * RWKV_WIDTH]
    lo = 3 * RWKV_WIDTH
    wa = rw[:, lo:lo + LORA_COLS]
    gd = rw[:, lo + LORA_COLS:lo + LORA_COLS + GATE_LORA]
    lane = lax.broadcasted_iota(jnp.int32, wa.shape, 1)
    wa = jnp.where(lane < LORA_COLS // 2, jnp.tanh(wa), wa)
    lora = _dot3(wa, lora_ref[...])
    zw = -(w0_ref[...] + lora[:, 0:RWKV_WIDTH])
    softplus = jnp.maximum(zw, 0.0) + jnp.log1p(jnp.exp(-jnp.abs(zw)))
    w = -softplus - 0.5
    a = jax.nn.sigmoid(a0_ref[...] + lora[:, RWKV_WIDTH:2 * RWKV_WIDTH])
    g = _dot3(jax.nn.sigmoid(gd), gup_ref[...])
    kk = k * kk_ref[...]
    ss = _dot_split(kk * kk, _group_ones(RWKV_WIDTH, HEAD_SIZE).astype(BF16), 3)
    kk = kk / jnp.maximum(jnp.sqrt(ss), 1e-12)
    k2 = k * (1.0 + (a - 1.0) * ka_ref[...])
    if has_vres:
        base = RWKV_COLS + POOL_WIDTH
        vd = shifted(base, base + VRES_PAD, vmu_ref[...])
        mixv = jax.nn.sigmoid(v0_ref[...] + _dot3(vd, vup_ref[...]))
        v = v + (vfirst_ref[...] - v) * mixv
    r_out[...] = r
    k_out[...] = k2
    v_out[...] = v
    lw_out[...] = -jnp.exp(w)
    kkn_out[...] = kk
    a_out[...] = a
    g_out[...] = g
    p_out[...] = proj[:, RWKV_COLS:RWKV_COLS + POOL_WIDTH]


def _premix_call(x2, modl, ln_g, w_full, mu, w0, a0, lora_w, g_up, k_k, k_a, vres, tiles_per_batch):
    T, D = x2.shape
    NC = w_full.shape[1]
    has_vres = vres is not None
    row = lambda n: pl.BlockSpec((1, n), lambda i: (0, 0))
    full = lambda a: pl.BlockSpec(a.shape, lambda i: (0,) * a.ndim)
    tok = lambda n: pl.BlockSpec((TM, n), lambda i: (i, 0))
    in_specs = [tok(D), pl.BlockSpec((1, 6, D), lambda i: (i // tiles_per_batch, 0, 0)), row(D),
                full(w_full), row(RWKV_COLS), row(RWKV_WIDTH), row(RWKV_WIDTH), full(lora_w), full(g_up),
                row(RWKV_WIDTH), row(RWKV_WIDTH)]
    args = [x2, modl, ln_g, w_full, mu, w0, a0, lora_w, g_up, k_k, k_a]
    if has_vres:
        vmu, v0, vup, vfirst = vres
        in_specs += [row(VRES_PAD), row(RWKV_WIDTH), full(vup), tok(RWKV_WIDTH)]
        args += [vmu, v0, vup, vfirst]
    out = jax.ShapeDtypeStruct((T, RWKV_WIDTH), F32)
    return pl.pallas_call(
        functools.partial(_premix_kernel, tiles_per_batch=tiles_per_batch, has_vres=has_vres),
        grid=(T // TM,),
        in_specs=in_specs,
        out_specs=[tok(RWKV_WIDTH)] * 8,
        out_shape=[out] * 8,
        scratch_shapes=[pltpu.VMEM((1, NC), F32)],
        compiler_params=pltpu.CompilerParams(dimension_semantics=("arbitrary",),
                                             vmem_limit_bytes=VMEM_LIMIT),
        name="premix",
    )(*args)


def _rwkv_kernel(r_ref, k_ref, v_ref, lw_ref, kk_ref, a_ref, g_ref, rk_ref, lng_ref, lnb_ref,
                 o_ref, state_ref, y_ref, ra_ref, yu_ref, xbk_ref, xv_ref, gc_ref):
    @pl.when(pl.program_id(1) == 0)
    def _():
        state_ref[...] = jnp.zeros_like(state_ref)

    n_seq, ts, width = r_ref.shape
    n_pairs = width // PAIR
    n_chunks = ts // CHUNK
    ti = lax.broadcasted_iota(jnp.int32, (CHUNK, CHUNK), 0)
    si = lax.broadcasted_iota(jnp.int32, (CHUNK, CHUNK), 1)
    tril = (ti >= si).astype(BF16)
    ri = lax.broadcasted_iota(jnp.int32, (2 * PAIR, 2 * PAIR), 0)
    ci = lax.broadcasted_iota(jnp.int32, (2 * PAIR, 2 * PAIR), 1)
    keep = (ri % CHUNK) + (ri >= PAIR).astype(jnp.int32) > (ci % CHUNK)
    ei = lax.broadcasted_iota(jnp.int32, (PAIR, PAIR), 0)
    ej = lax.broadcasted_iota(jnp.int32, (PAIR, PAIR), 1)
    eye = (ei == ej).astype(F32)
    lane = lax.broadcasted_iota(jnp.int32, (CHUNK, PAIR), 1)
    head0 = lane < HEAD_SIZE
    zeros_pp = jnp.zeros((PAIR, PAIR), BF16)

    def stack(t):
        return jnp.concatenate([jnp.where(head0, t, 0.0), jnp.where(head0, 0.0, t)], axis=0).astype(BF16)

    def pre_body(b, carry):
        chains = []
        for cc in range(n_chunks):
            c = b * n_chunks + cc
            rows = slice(cc * CHUNK, (cc + 1) * CHUNK)
            lw = lw_ref[b, rows, :]
            cum = _dot_split(tril, lw, 3, split_lhs=False)
            g_in = jnp.exp(cum)
            g_ex = jnp.exp(cum - lw)
            g_inv = jnp.exp(-cum)
            gc_ref[c] = g_in[CHUNK - 1:CHUNK, :]
            kk = kk_ref[b, rows, :]
            rt = r_ref[b, rows, :] * g_in
            kt = k_ref[b, rows, :] * g_inv
            bt = (kk * a_ref[b, rows, :]) * g_inv
            at = -kk * g_ex
            vv = v_ref[b, rows, :]
            for p in range(n_pairs):
                ls = slice(p * PAIR, (p + 1) * PAIR)
                xa, xb, xk, xr, xv = (stack(t[:, ls]) for t in (at, bt, kt, rt, vv))
                xbk = jnp.concatenate([xb, xk], axis=0)
                xbk_ref[c, p] = xbk
                xv_ref[c, p] = xv
                chains.append(dict(c=c, p=p, xa=xa, xr=xr, xv=xv, xbk=xbk))
        for ch in chains:
            big = _dot(jnp.concatenate([ch["xa"], ch["xr"]], axis=0), ch["xbk"], _NT)
            big = jnp.where(keep, big, 0.0)
            ch["lab"] = big[0:PAIR, 0:PAIR]
            ch["lak"] = big[0:PAIR, PAIR:2 * PAIR].astype(BF16)
            ch["lrbk"] = big[PAIR:2 * PAIR, :].astype(BF16)
        for ch in chains:
            lb = ch["lab"].astype(BF16)
            ch["pw"] = _dot(lb, lb)
            ch["lakv"] = _dot(ch["lak"], ch["xv"]).astype(BF16)
            ch["inv"] = eye + ch["lab"]
        for i in range(1, 6):
            for ch in chains:
                pb = ch["pw"].astype(BF16)
                sb = ch["inv"].astype(BF16)
                if i < 5:
                    res = _dot(pb, jnp.concatenate([pb, sb], axis=1))
                    ch["pw"] = res[:, 0:PAIR]
                    ch["inv"] = ch["inv"] + res[:, PAIR:2 * PAIR]
                else:
                    ch["inv"] = ch["inv"] + _dot(pb, sb)
        for ch in chains:
            au = _dot(ch["inv"].astype(BF16), jnp.concatenate([ch["xa"], ch["lakv"]], axis=1))
            ch["a2"] = au[:, 0:PAIR].astype(BF16)
            ch["uv"] = au[:, PAIR:2 * PAIR]
        for ch in chains:
            rhs = jnp.concatenate(
                [jnp.concatenate([ch["a2"], ch["uv"].astype(BF16)], axis=1),
                 jnp.concatenate([zeros_pp, ch["xv"]], axis=1)], axis=0)
            ry = _dot(ch["lrbk"], rhs)
            r2 = ch["xr"].astype(F32) + ry[:, 0:PAIR]
            ra_ref[ch["c"], ch["p"]] = jnp.concatenate([r2.astype(BF16), ch["a2"]], axis=0)
            yu_ref[ch["c"], ch["p"]] = jnp.concatenate([ry[:, PAIR:2 * PAIR], ch["uv"]], axis=0)
        return carry

    lax.fori_loop(0, n_seq, pre_body, 0)

    def seq_body(cc, carry):
        rows = pl.ds(pl.multiple_of(cc * CHUNK, CHUNK), CHUNK)
        chains = [(b, p, b * n_chunks + cc) for b in range(n_seq) for p in range(n_pairs)]
        res = [_dot(ra_ref[c, p], state_ref[b, p].astype(BF16), _NT) + yu_ref[c, p] for b, p, c in chains]
        for (b, p, c), res_bp in zip(chains, res):
            ls = slice(p * PAIR, (p + 1) * PAIR)
            y_st = res_bp[0:PAIR]
            u_st = res_bp[PAIR:2 * PAIR].astype(BF16)
            st = _dot(jnp.concatenate([u_st, xv_ref[c, p]], axis=0), xbk_ref[c, p], _TN)
            state_ref[b, p] = (state_ref[b, p] + st) * gc_ref[c][:, ls]
            y_ref[b, rows, ls] = y_st[0:CHUNK] + y_st[CHUNK:2 * CHUNK]
        return carry

    lax.fori_loop(0, n_chunks, seq_body, 0)

    ones = _group_ones(RWKV_WIDTH, HEAD_SIZE).astype(BF16)
    flat = lambda ref: ref[...].reshape(n_seq * ts, width)
    y = flat(y_ref)
    mean = _dot_split(y, ones, 2) * (1.0 / HEAD_SIZE)
    d = y - mean
    var = _dot_split(d * d, ones, 2) * (1.0 / HEAD_SIZE)
    yn = d * lax.rsqrt(var + GN_EPS) * lng_ref[...] + lnb_ref[...]
    bonus = _dot_split(flat(r_ref) * flat(k_ref) * rk_ref[...], ones, 2) * flat(v_ref)
    o_ref[...] = ((yn + bonus) * flat(g_ref)).reshape(n_seq, ts, width)


def _rwkv_call(r, k, v, lw, kk, a, g, r_k, lnx_g, lnx_b, batch):
    T, W = r.shape
    seq = T // batch
    ts = TS // RWKV_SEQS_PER_STEP
    n_chunks, n_pairs = TS // CHUNK, W // PAIR
    tok = pl.BlockSpec((RWKV_SEQS_PER_STEP, ts, W), lambda b, j: (b, j, 0))
    row = pl.BlockSpec((1, W), lambda b, j: (0, 0))
    per_seq = lambda t: t.reshape(batch, seq, W)
    return pl.pallas_call(
        _rwkv_kernel,
        grid=(batch // RWKV_SEQS_PER_STEP, seq // ts),
        in_specs=[tok] * 7 + [row] * 3,
        out_specs=tok,
        out_shape=jax.ShapeDtypeStruct((batch, seq, W), F32),
        scratch_shapes=[pltpu.VMEM((RWKV_SEQS_PER_STEP, n_pairs, PAIR, PAIR), F32),
                        pltpu.VMEM((RWKV_SEQS_PER_STEP, ts, W), F32),
                        pltpu.VMEM((n_chunks, n_pairs, 2 * PAIR, PAIR), BF16),
                        pltpu.VMEM((n_chunks, n_pairs, 2 * PAIR, PAIR), F32),
                        pltpu.VMEM((n_chunks, n_pairs, 2 * PAIR, PAIR), BF16),
                        pltpu.VMEM((n_chunks, n_pairs, PAIR, PAIR), BF16),
                        pltpu.VMEM((n_chunks, 1, W), F32)],
        compiler_params=pltpu.CompilerParams(dimension_semantics=("arbitrary", "arbitrary"),
                                             vmem_limit_bytes=VMEM_LIMIT),
        name="rwkv",
    )(*(per_seq(t) for t in (r, k, v, lw, kk, a, g)), r_k, lnx_g, lnx_b).reshape(T, W)


def _postmix_kernel(x_ref, y_ref, p_ref, mod_ref, pw_ref, ps_ref, wo_ref, ln_ref,
                    x_out, ht_out, carry_ref, *, tiles_per_batch):
    i = pl.program_id(0)
    j = i % tiles_per_batch

    @pl.when(j == 0)
    def _():
        carry_ref[...] = jnp.zeros_like(carry_ref)

    m = mod_ref[0]
    p = p_ref[...]
    tm = p.shape[0]
    ext = jnp.concatenate([carry_ref[...], p], axis=0)
    carry_ref[...] = p[tm - MAX_WINDOW:tm, :]
    pos = (j * tm + 1 + lax.broadcasted_iota(jnp.int32, (tm, 1), 0)).astype(F32)
    pooled = []
    for gi, win in enumerate(POOL_WINDOWS):
        ls = slice(gi * POOL_GROUP, (gi + 1) * POOL_GROUP)
        wsum = ext[:, ls]
        span = 1
        while span < win:
            wsum = wsum + pltpu.roll(wsum, span, 0)
            span *= 2
        mean = wsum[MAX_WINDOW:, :] / jnp.minimum(pos, float(win))
        d = mean - p[:, ls]
        pooled.append(_dot3(d, pw_ref[gi]))
    y_pool = jnp.concatenate(pooled, axis=1) * ps_ref[...]
    mix = (_dot(y_ref[...].astype(BF16), wo_ref[0:RWKV_WIDTH, :])
           + _dot(y_pool.astype(BF16), wo_ref[RWKV_WIDTH:RWKV_WIDTH + POOL_WIDTH, :]))
    x = x_ref[...] + m[2:3] * mix
    x_out[...] = x
    h2 = _rms_mod(x, ln_ref[...], m[3:4], m[4:5])
    ht_out[...] = h2.T.astype(BF16)


def _postmix_call(x2, y_rwkv, pool_in, modl, pool_w, pool_scale, w_out, ln_g, tiles_per_batch):
    T, D = x2.shape
    row = lambda n: pl.BlockSpec((1, n), lambda i: (0, 0))
    full = lambda a: pl.BlockSpec(a.shape, lambda i: (0,) * a.ndim)
    tok = lambda n: pl.BlockSpec((TM, n), lambda i: (i, 0))
    return pl.pallas_call(
        functools.partial(_postmix_kernel, tiles_per_batch=tiles_per_batch),
        grid=(T // TM,),
        in_specs=[tok(D), tok(RWKV_WIDTH), tok(POOL_WIDTH),
                  pl.BlockSpec((1, 6, D), lambda i: (i // tiles_per_batch, 0, 0)),
                  full(pool_w), row(POOL_WIDTH), full(w_out), row(D)],
        out_specs=[tok(D), pl.BlockSpec((D, TM), lambda i: (0, i))],
        out_shape=[jax.ShapeDtypeStruct((T, D), F32), jax.ShapeDtypeStruct((D, T), BF16)],
        scratch_shapes=[pltpu.VMEM((MAX_WINDOW, POOL_WIDTH), F32)],
        compiler_params=pltpu.CompilerParams(dimension_semantics=("arbitrary",),
                                             vmem_limit_bytes=VMEM_LIMIT),
        name="postmix",
    )(x2, y_rwkv, pool_in, modl, pool_w, pool_scale, w_out, ln_g)


def _extract16(cur, mark, marks, tie_break):
    n, w = cur.shape
    iota = lax.broadcasted_iota(jnp.int32, (n, w), 0).astype(F32)
    vals = []
    for j in range(PEER_TOPK):
        mx = jnp.max(cur, axis=0, keepdims=True)
        hit = cur == mx
        if tie_break:
            first = jnp.min(jnp.where(hit, iota, float(n)), axis=0, keepdims=True)
            hit = iota == first
        mark = jnp.where(hit, marks[j], mark)
        cur = jnp.where(hit, -jnp.inf, cur)
        vals.append(mx)
    removed = jnp.sum((cur == -jnp.inf).astype(F32), axis=0, keepdims=True)
    return mark, vals, removed


def _pair_stage(vals1, v2_ref, cand_ref, bm_ref, tie_break):
    w = v2_ref.shape[1]
    for (a, nb), off in zip(_CAND_ROWS, _CAND_OFFS):
        cand_ref[off:off + nb, :] = vals1[a] + v2_ref[0:nb, :]
    cand_ref[N_CAND:N_CAND_PAD, :] = jnp.full((N_CAND_PAD - N_CAND, w), -jnp.inf, F32)
    cand = cand_ref[...]
    chosen, _, removed = _extract16(cand, jnp.zeros(cand.shape, F32), [1.0] * PEER_TOPK, tie_break)
    z = jnp.sum(chosen * jnp.exp(cand - cand[0:1, :]), axis=0, keepdims=True)
    cand_ref[...] = chosen
    for (a, nb), off in zip(_CAND_ROWS, _CAND_OFFS):
        bm_ref[a:a + 1, :] = jnp.sum(cand_ref[off:off + nb, :], axis=0, keepdims=True)
    return z, removed - float(N_CAND_PAD - N_CAND)


def _select_strip_exact(s1, s2, outs, head, lanes, v2_ref, cand_ref, bm_ref):
    rank2_out, e2_out, b_out, f_out = outs
    ranks = [float(j + 1) for j in range(PEER_TOPK)]
    unranked = jnp.full(s1.shape, 99.0, F32)
    rank1, vals1, _ = _extract16(s1, unranked, ranks, True)
    rank2, vals2, _ = _extract16(s2, unranked, ranks, True)
    for j in range(PEER_TOPK):
        v2_ref[j:j + 1, :] = vals2[j]
    z, _ = _pair_stage(vals1, v2_ref, cand_ref, bm_ref, True)
    bsel = jnp.zeros(s1.shape, F32)
    for a in range(PEER_TOPK):
        bsel = jnp.where(rank1 == ranks[a], bm_ref[a:a + 1, :], bsel)
    rank2_out[head, :, lanes] = rank2.astype(BF16)
    e2_out[head, :, lanes] = jnp.exp(s2 - vals2[0]).astype(BF16)
    b_out[head, :, lanes] = bsel
    f_out[head, :, lanes] = jnp.exp(s1 - vals1[0]) / z


def _oddeven_merge(lo, hi, r):
    step = r * 2
    if step < hi - lo:
        yield from _oddeven_merge(lo, hi, step)
        yield from _oddeven_merge(lo + r, hi, step)
        yield from [(i, i + r) for i in range(lo + r, hi - r, step)]
    else:
        yield (lo, lo + r)


def _oddeven_sort(lo, hi):
    if hi - lo >= 1:
        mid = lo + (hi - lo) // 2
        yield from _oddeven_sort(lo, mid)
        yield from _oddeven_sort(mid + 1, hi)
        yield from _oddeven_merge(lo, hi, 1)


def _exchange(a, i, j):
    a[i], a[j] = jnp.maximum(a[i], a[j]), jnp.minimum(a[i], a[j])


def _sorted_top16(slabs):
    a = list(slabs)
    for i, j in _oddeven_sort(0, len(a) - 1):
        _exchange(a, i, j)
    for shift in (4, 2, 1):
        other = [pltpu.roll(t, shift, 0) for t in a]
        if len(a) < PEER_TOPK:
            a = a + other[::-1]
        else:
            a = [jnp.maximum(a[j], other[PEER_TOPK - 1 - j]) for j in range(PEER_TOPK)]
        d = PEER_TOPK // 2
        while d >= 1:
            for i in range(PEER_TOPK):
                if i & d == 0:
                    _exchange(a, i, i + d)
            d //= 2
    return a


def _count_above(x, v):
    c1 = v[7] > x
    c2 = jnp.where(c1, v[11], v[3]) > x
    c3 = jnp.where(c1, jnp.where(c2, v[13], v[9]), jnp.where(c2, v[5], v[1])) > x
    c4 = jnp.where(c1,
                   jnp.where(c2, jnp.where(c3, v[14], v[12]), jnp.where(c3, v[10], v[8])),
                   jnp.where(c2, jnp.where(c3, v[6], v[4]), jnp.where(c3, v[2], v[0]))) > x
    return c1, c2, c3, c4


def _pick16(masks, table):
    t = list(table)
    for c in reversed(masks):
        t = [jnp.where(c, t[2 * k + 1], t[2 * k]) for k in range(len(t) // 2)]
    return t[0]


def _select_strip_fast(s1, s2, outs, head, lanes, v2_ref, cand_ref, bm_ref):
    rank2_out, e2_out, b_out, f_out = outs
    w = s1.shape[1]
    slab_shape = (SUBLANES, w)
    slabs_of = lambda t: [t[SUBLANES * k:SUBLANES * (k + 1), :] for k in range(t.shape[0] // SUBLANES)]
    slabs1 = slabs_of(s1)
    slabs2 = slabs_of(s2)
    n_slabs = len(slabs1)
    top1 = _sorted_top16(slabs1)
    top2 = _sorted_top16(slabs2)
    for j in range(PEER_TOPK):
        v2_ref[j:j + 1, :] = top2[j][0:1, :]
    for (a, nb), off in zip(_CAND_ROWS, _CAND_OFFS):
        cand_ref[off:off + nb, :] = top1[a][0:1, :] + v2_ref[0:nb, :]
    cand_ref[N_CAND:N_CAND_PAD, :] = jnp.full((N_CAND_PAD - N_CAND, w), -jnp.inf, F32)
    cand = cand_ref[...]
    cand_slabs = slabs_of(cand) + [jnp.full(slab_shape, -jnp.inf, F32)]
    tau = _sorted_top16(cand_slabs)[PEER_TOPK - 1]
    chosen = cand >= tau[0:1, :]
    z = jnp.sum(jnp.where(chosen, jnp.exp(cand - cand[0:1, :]), 0.0), axis=0, keepdims=True)
    removed = jnp.sum(jnp.where(chosen, 1.0, 0.0), axis=0, keepdims=True)
    counts = []
    for a, nb in _CAND_ROWS:
        hits = [jnp.where(top1[a] + top2[b] >= tau, 1.0, 0.0) for b in range(nb)]
        counts.append(functools.reduce(lambda p, q: p + q, hits))
    zb = jnp.broadcast_to(z, slab_shape)
    suspect = jnp.zeros(slab_shape, F32)
    for j in range(PEER_TOPK - 1):
        suspect = jnp.where(top1[j] == top1[j + 1], 99.0, suspect)
        suspect = jnp.where(top2[j] == top2[j + 1], 99.0, suspect)
    in_top1 = jnp.zeros(slab_shape, F32)
    in_top2 = jnp.zeros(slab_shape, F32)
    rank2, e2, bsel, fsel = [], [], [], []
    for k in range(n_slabs):
        x1, x2 = slabs1[k], slabs2[k]
        keep1 = x1 >= top1[PEER_TOPK - 1]
        keep2 = x2 >= top2[PEER_TOPK - 1]
        in_top1 = in_top1 + jnp.where(keep1, 1.0, 0.0)
        in_top2 = in_top2 + jnp.where(keep2, 1.0, 0.0)
        bsel.append(jnp.where(keep1, _pick16(_count_above(x1, top1), counts), 0.0))
        c1, c2, c3, c4 = _count_above(x2, top2)
        rank = (1.0 + jnp.where(c1, 8.0, 0.0)) + (jnp.where(c2, 4.0, 0.0)
                                                  + (jnp.where(c3, 2.0, 0.0) + jnp.where(c4, 1.0, 0.0)))
        rank2.append(jnp.where(keep2, rank, 99.0))
        e2.append(jnp.exp(x2 - top2[0]))
        fsel.append(jnp.exp(x1 - top1[0]) / zb)
    rank2_out[head, :, lanes] = jnp.concatenate(rank2, axis=0).astype(BF16)
    e2_out[head, :, lanes] = jnp.concatenate(e2, axis=0).astype(BF16)
    b_out[head, :, lanes] = jnp.concatenate(bsel, axis=0)
    f_out[head, :, lanes] = jnp.concatenate(fsel, axis=0)
    kept = jnp.maximum(jnp.sum(in_top1, axis=0, keepdims=True), jnp.sum(in_top2, axis=0, keepdims=True))
    return jnp.maximum(jnp.max(jnp.maximum(kept, removed)), jnp.max(suspect))


def _keyproj_kernel(keys_ref, q_ref, o_ref):
    o_ref[...] = _dot(keys_ref[0], q_ref[...], _NT, precision=HIGHEST).astype(BF16)


def _keyproj_call(keys, peer_q):
    D = peer_q.shape[0]
    n = keys.shape[0] * keys.shape[1]
    return pl.pallas_call(
        _keyproj_kernel,
        grid=(n,),
        in_specs=[pl.BlockSpec((1, N_KEYS, N_KEYS), lambda i: (i, 0, 0)),
                  pl.BlockSpec((D, N_KEYS), lambda i: (0, i))],
        out_specs=pl.BlockSpec((N_KEYS, D), lambda i: (i, 0)),
        out_shape=jax.ShapeDtypeStruct((n * N_KEYS, D), BF16),
        name="keyproj",
    )(keys.reshape(n, N_KEYS, N_KEYS), peer_q)


def _select_kernel(ht_ref, ws_ref, rank2_out, e2_out, b_out, f_out,
                   s_ref, v2_ref, cand_ref, bm_ref):
    s_ref[...] = _dot(ws_ref[...], ht_ref[...])
    outs = (rank2_out, e2_out, b_out, f_out)
    n_strips = s_ref.shape[1] // SELECT_STRIP

    def strip(idx, carry):
        head = idx // n_strips
        lanes = pl.ds(pl.multiple_of((idx % n_strips) * SELECT_STRIP, SELECT_STRIP), SELECT_STRIP)
        s1 = s_ref[pl.ds(pl.multiple_of(head * 2 * N_KEYS, N_KEYS), N_KEYS), lanes]
        s2 = s_ref[pl.ds(pl.multiple_of(head * 2 * N_KEYS + N_KEYS, N_KEYS), N_KEYS), lanes]
        maybe_tied = _select_strip_fast(s1, s2, outs, head, lanes, v2_ref, cand_ref, bm_ref)

        @pl.when(maybe_tied > float(PEER_TOPK))
        def _():
            _select_strip_exact(s1, s2, outs, head, lanes, v2_ref, cand_ref, bm_ref)

        return carry

    lax.fori_loop(0, SELECT_HEADS * n_strips, strip, 0)


def _select_call(h2t, ws):
    D, T = h2t.shape
    out = jax.ShapeDtypeStruct((PEER_HEADS, N_KEYS, T), F32)
    out16 = jax.ShapeDtypeStruct((PEER_HEADS, N_KEYS, T), BF16)
    ospec = pl.BlockSpec((SELECT_HEADS, N_KEYS, TQ), lambda i, h: (h, 0, i))
    return pl.pallas_call(
        _select_kernel,
        grid=(T // TQ, PEER_HEADS // SELECT_HEADS),
        in_specs=[pl.BlockSpec((D, TQ), lambda i, h: (0, i)),
                  pl.BlockSpec((SELECT_HEADS * 2 * N_KEYS, D), lambda i, h: (h, 0))],
        out_specs=[ospec] * 4,
        out_shape=[out16, out16, out, out],
        scratch_shapes=[pltpu.VMEM((SELECT_HEADS * 2 * N_KEYS, TQ), F32),
                        pltpu.VMEM((PEER_TOPK, SELECT_STRIP), F32),
                        pltpu.VMEM((N_CAND_PAD, SELECT_STRIP), F32),
                        pltpu.VMEM((PEER_TOPK, SELECT_STRIP), F32)],
        compiler_params=pltpu.CompilerParams(dimension_semantics=("arbitrary", "arbitrary"),
                                             vmem_limit_bytes=VMEM_LIMIT),
        name="select",
    )(h2t, ws)


def _transpose_cast_kernel(x_ref, o_ref):
    o_ref[...] = x_ref[0].T.astype(BF16)


def _transpose_cast_call(w, layer):
    _, n, d = w.shape
    return pl.pallas_call(
        _transpose_cast_kernel,
        grid=(n // EB,),
        in_specs=[pl.BlockSpec((1, EB, d), lambda i: (layer, i, 0))],
        out_specs=pl.BlockSpec((d, EB), lambda i: (0, i)),
        out_shape=jax.ShapeDtypeStruct((d, n), BF16),
        name="transpose_cast",
    )(w)


def _experts_kernel(ht_ref, u_ref, vt_ref, rank2_ref, e2_ref, b_ref, f_ref, x_ref, mod_ref, lnf_ref,
                    x_out, acc_ref, *, tiles_per_batch, final):
    e = pl.program_id(1)

    @pl.when(e == 0)
    def _():
        acc_ref[...] = jnp.zeros_like(acc_ref)

    tt = ht_ref.shape[1]

    gates = []
    for ii in range(FIRST_KEYS_PER_STEP):
        gate = None
        for h in range(PEER_HEADS):
            brow = jnp.broadcast_to(b_ref[h, 0, ii:ii + 1, :], (N_KEYS, tt)).astype(BF16)
            frow = jnp.broadcast_to(f_ref[h, 0, ii:ii + 1, :], (N_KEYS, tt)).astype(BF16)
            term = jnp.where(brow >= rank2_ref[h], e2_ref[h], 0.0) * frow
            gate = term if gate is None else gate + term
        gates.append(gate)
    act = _dot(u_ref[...], ht_ref[...])
    gelu = 0.5 * act.astype(BF16) * (1.0 + lax.erf((act * (2.0 ** -0.5)).astype(BF16)))
    z = jnp.concatenate(gates, axis=0) * gelu
    acc_ref[...] += _dot(vt_ref[...], z)

    @pl.when(e == pl.num_programs(1) - 1)
    def _():
        m = mod_ref[0]
        x = x_ref[...] + m[5:6] * acc_ref[...].T
        if final:
            x = x * lax.rsqrt(jnp.mean(x * x, axis=-1, keepdims=True) + NORM_EPS) * lnf_ref[...]
        x_out[...] = x


def _experts_call(h2t, u_bf, vt_bf, rank2, e2, bsel, fsel, x2, modl, lnf_g, tiles_per_batch, final):
    D, T = h2t.shape
    NE = u_bf.shape[0]
    sel = pl.BlockSpec((PEER_HEADS, N_KEYS, TT), lambda i, e: (0, 0, i))
    bsel, fsel = (t.reshape(PEER_HEADS, N_KEYS // FIRST_KEYS_PER_STEP, FIRST_KEYS_PER_STEP, T)
                  for t in (bsel, fsel))
    sel1 = pl.BlockSpec((PEER_HEADS, 1, FIRST_KEYS_PER_STEP, TT), lambda i, e: (0, e, 0, i))
    tok = pl.BlockSpec((TT, D), lambda i, e: (i, 0))
    return pl.pallas_call(
        functools.partial(_experts_kernel, tiles_per_batch=tiles_per_batch, final=final),
        grid=(T // TT, NE // EB),
        in_specs=[pl.BlockSpec((D, TT), lambda i, e: (0, i)),
                  pl.BlockSpec((EB, D), lambda i, e: (e, 0)),
                  pl.BlockSpec((D, EB), lambda i, e: (0, e)),
                  sel, sel, sel1, sel1, tok,
                  pl.BlockSpec((1, 6, D), lambda i, e: (i // tiles_per_batch, 0, 0)),
                  pl.BlockSpec((1, D), lambda i, e: (0, 0))],
        out_specs=tok,
        out_shape=jax.ShapeDtypeStruct((T, D), F32),
        scratch_shapes=[pltpu.VMEM((D, TT), F32)],
        compiler_params=pltpu.CompilerParams(dimension_semantics=("arbitrary", "arbitrary"),
                                             vmem_limit_bytes=VMEM_LIMIT),
        name="experts",
    )(h2t, u_bf, vt_bf, rank2, e2, bsel, fsel, x2, modl, lnf_g)


def kernel(x, c, ada_w, ada_b, ln1_g, w_in, mu_shift, w0, w_up, a0, a_up, g_up, vres_down, vres_mu,
           vres_v0, vres_up, k_k, k_a, r_k, lnx_g, lnx_b, pool_w, pool_scale, w_out, ln2_g, peer_q,
           peer_keys, peer_u, peer_v, lnf_g):
    B, S, D = x.shape
    L = ada_w.shape[0]
    T = B * S
    assert S % TS == 0 and S % TM == 0 and T % TQ == 0 and T % TT == 0 and TT <= S
    assert B % RWKV_SEQS_PER_STEP == 0
    row = lambda t: t.reshape(1, -1)

    c_pad = jnp.zeros((SUBLANES, D), F32).at[:B].set(c)
    mod = _ada_call(c_pad, ada_w, ada_b)[:, :B].reshape(L, B, 6, D)

    half = LORA_COLS // 2
    x2 = x.reshape(T, D)
    v_first = None
    for l in range(L):
        if l == 0:
            w_full = w_in[l].astype(BF16)
            vres = None
        else:
            mv = vres_down.shape[-1]
            w_full = jnp.concatenate(
                [w_in[l], vres_down[l - 1], jnp.zeros((D, VRES_PAD - mv), F32)], axis=1).astype(BF16)
            vres = (jnp.zeros((1, VRES_PAD), F32).at[:, :mv].set(vres_mu[l - 1]),
                    row(vres_v0[l - 1]),
                    jnp.zeros((VRES_PAD, RWKV_WIDTH), F32).at[:mv].set(vres_up[l - 1]),
                    v_first)
        lora_w = jnp.zeros((LORA_COLS, 2 * RWKV_WIDTH), F32)
        lora_w = lora_w.at[:half, :RWKV_WIDTH].set(w_up[l]).at[half:, RWKV_WIDTH:].set(a_up[l])
        r, k2, v, lw, kk, a, g, pool_in = _premix_call(
            x2, mod[l], row(ln1_g[l]), w_full, row(mu_shift[l]), row(w0[l]), row(a0[l]), lora_w,
            g_up[l], row(k_k[l]), row(k_a[l]), vres, S // TM)
        if l == 0:
            v_first = v
        y_rwkv = _rwkv_call(r, k2, v, lw, kk, a, g, row(r_k[l]), row(lnx_g[l]), row(lnx_b[l]), B)
        x_mid, h2t = _postmix_call(x2, y_rwkv, pool_in, mod[l], pool_w[l], row(pool_scale[l]),
                                   w_out[l].astype(BF16), row(ln2_g[l]), S // TM)
        ws = _keyproj_call(peer_keys[l], peer_q[l])
        rank2, e2, bsel, fsel = _select_call(h2t, ws)
        x2 = _experts_call(h2t, peer_u[l].astype(BF16), _transpose_cast_call(peer_v, l), rank2, e2, bsel, fsel,
                           x_mid, mod[l], row(lnf_g), S // TT, l == L - 1)
    return x2.reshape(B, S, D)
```
